```python
import jax, jax.numpy as jnp
from jax import lax
import numpy as np

D_MODEL = 1024
BATCH = 4
SEQ = 4096
DEPTH = 1

MEM_LEN = 256
MIX_WIDTH = D_MODEL
SGU_WIDTH = MIX_WIDTH // 2
SGU_GROUPS = 4
SGU_GROUP_DIM = SGU_WIDTH // SGU_GROUPS
CHUNK = 128
SB_WIDTH = MIX_WIDTH - SGU_WIDTH
SB_HEAD_DIM = 64
SB_HEADS = SB_WIDTH // SB_HEAD_DIM
Q_BLOCK = 128
XA_HEADS = 4
XA_HEAD_DIM = D_MODEL // XA_HEADS
D_FF = ((8 * D_MODEL // 3 + 127) // 128) * 128
IN_COLS = 2 * SGU_WIDTH + 3 * SB_WIDTH
EPS = 1e-6

kernel_name = "hybrid_sgu_stickbreaking_macaron_block"


def rmsnorm(x, g):
    xf = x.astype(jnp.float32)
    y = xf * lax.rsqrt(jnp.mean(xf * xf, axis=-1, keepdims=True) + EPS)
    return (y * g.astype(jnp.float32)).astype(x.dtype)


def swiglu(x, w_gate, w_up, w_down):
    return (jax.nn.silu(x @ w_gate) * (x @ w_up)) @ w_down


def chunked_sgu(u, v, norm_g, norm_b, w_s, b_s):
    B, S, G, Dg = v.shape
    vf = v.astype(jnp.float32)
    mu = jnp.mean(vf, axis=-1, keepdims=True)
    var = jnp.mean((vf - mu) ** 2, axis=-1, keepdims=True)
    vn = ((vf - mu) * lax.rsqrt(var + EPS) * norm_g.astype(jnp.float32)
          + norm_b.astype(jnp.float32)).astype(v.dtype)
    vc = vn.reshape(B, S // CHUNK, CHUNK, G, Dg)
    causal = jnp.tril(jnp.ones((CHUNK, CHUNK), dtype=bool))
    w = jnp.where(causal[None], w_s, jnp.zeros_like(w_s)).astype(v.dtype)
    mixed = jnp.einsum('gts,bcsgd->bctgd', w, vc) + b_s.T.astype(v.dtype)[None, None, :, :, None]
    return u * mixed.reshape(B, S, G, Dg)


def stick_breaking_attention(q, k, v):
    B, H, S, Dh = q.shape
    scale = Dh ** -0.5
    outs = []
    for blk in range(S // Q_BLOCK):
        t0 = blk * Q_BLOCK
        t1 = t0 + Q_BLOCK
        qb = q[:, :, t0:t1]
        kb = k[:, :, :t1]
        vb = v[:, :, :t1]
        z = jnp.einsum('bhtd,bhsd->bhts', qb, kb,
                       preferred_element_type=jnp.float32) * scale
        t_idx = t0 + jnp.arange(Q_BLOCK)[:, None]
        s_idx = jnp.arange(t1)[None, :]
        strict = s_idx < t_idx
        log_beta = jax.nn.log_sigmoid(z)
        log_1m = jnp.where(strict, jax.nn.log_sigmoid(-z), 0.0)
        suffix = lax.cumsum(log_1m, axis=log_1m.ndim - 1, reverse=True) - log_1m
        a = jnp.where(strict, jnp.exp(log_beta + suffix), 0.0)
        outs.append(jnp.einsum('bhts,bhsd->bhtd', a.astype(v.dtype), vb))
    return jnp.concatenate(outs, axis=2)


def memory_cross_attention(x, memn, w_q, w_kv, w_o):
    B, S, D = x.shape
    M = memn.shape[1]
    q = (x @ w_q).reshape(B, S, XA_HEADS, XA_HEAD_DIM)
    kv = (memn @ w_kv).reshape(B, M, 2, XA_HEADS, XA_HEAD_DIM)
    k, v = kv[:, :, 0], kv[:, :, 1]
    logits = jnp.einsum('bshd,bmhd->bhsm', q, k,
                        preferred_element_type=jnp.float32) * (XA_HEAD_DIM ** -0.5)
    p = jax.nn.softmax(logits, axis=-1).astype(v.dtype)
    o = jnp.einsum('bhsm,bmhd->bshd', p, v).reshape(B, S, XA_HEADS * XA_HEAD_DIM)
    return o @ w_o


def setup_inputs(seed: int = 0) -> dict:
    key = jax.random.key(seed)
    ks = iter(jax.random.split(key, 40))
    L, D, F = DEPTH, D_MODEL, D_FF

    def nrm(shape, scale):
        return jax.random.normal(next(ks), shape, jnp.float32) * scale

    def gain(shape):
        return 1.0 + nrm(shape, 0.02)

    return {
        "x": nrm((BATCH, SEQ, D), 1.0),
        "mem": nrm((BATCH, MEM_LEN, D), 1.0),
        "ffn1_pre_g": gain((L, D)),
        "ffn1_post_g": gain((L, D)),
        "ffn1_w_gate": nrm((L, D, F), D ** -0.5),
        "ffn1_w_up": nrm((L, D, F), D ** -0.5),
        "ffn1_w_down": nrm((L, F, D), F ** -0.5),
        "mix_pre_g": gain((L, D)),
        "mix_post_g": gain((L, D)),
        "w_in": nrm((L, D, IN_COLS), D ** -0.5),
        "sgu_norm_g": gain((L, SGU_GROUPS, SGU_GROUP_DIM)),
        "sgu_norm_b": nrm((L, SGU_GROUPS, SGU_GROUP_DIM), 0.02),
        "sgu_w_s": nrm((L, SGU_GROUPS, CHUNK, CHUNK), CHUNK ** -0.5),
        "sgu_b_s": 1.0 + nrm((L, SGU_GROUPS, CHUNK), 0.02),
        "sgu_out_g": gain((L, SGU_WIDTH)),
        "sb_out_g": gain((L, SB_WIDTH)),
        "w_out": nrm((L, MIX_WIDTH, D), MIX_WIDTH ** -0.5),
        "xa_pre_g": gain((L, D)),
        "xa_post_g": gain((L, D)),
        "mem_norm_g": gain((L, D)),
        "xa_w_q": nrm((L, D, XA_HEADS * XA_HEAD_DIM), D ** -0.5),
        "xa_w_kv": nrm((L, D, 2 * XA_HEADS * XA_HEAD_DIM), D ** -0.5),
        "xa_w_o": nrm((L, XA_HEADS * XA_HEAD_DIM, D), (XA_HEADS * XA_HEAD_DIM) ** -0.5),
        "ffn2_pre_g": gain((L, D)),
        "ffn2_post_g": gain((L, D)),
        "ffn2_w_gate": nrm((L, D, F), D ** -0.5),
        "ffn2_w_up": nrm((L, D, F), D ** -0.5),
        "ffn2_w_down": nrm((L, F, D), F ** -0.5),
        "final_norm_g": gain((L, D)),
    }


def reference(x, mem, ffn1_pre_g, ffn1_post_g, ffn1_w_gate, ffn1_w_up, ffn1_w_down,
              mix_pre_g, mix_post_g, w_in, sgu_norm_g, sgu_norm_b, sgu_w_s, sgu_b_s,
              sgu_out_g, sb_out_g, w_out, xa_pre_g, xa_post_g, mem_norm_g, xa_w_q,
              xa_w_kv, xa_w_o, ffn2_pre_g, ffn2_post_g, ffn2_w_gate, ffn2_w_up,
              ffn2_w_down, final_norm_g):
    B, S, D = x.shape
    splits = [SGU_WIDTH, 2 * SGU_WIDTH, 2 * SGU_WIDTH + SB_WIDTH,
              2 * SGU_WIDTH + 2 * SB_WIDTH]
    h = x
    for l in range(DEPTH):
        f = swiglu(rmsnorm(h, ffn1_pre_g[l]), ffn1_w_gate[l], ffn1_w_up[l], ffn1_w_down[l])
        h = h + 0.5 * rmsnorm(f, ffn1_post_g[l])

        n = rmsnorm(h, mix_pre_g[l])
        proj = n @ w_in[l]
        u, vg, q, k, vs = jnp.split(proj, splits, axis=-1)
        u = jax.nn.gelu(u).reshape(B, S, SGU_GROUPS, SGU_GROUP_DIM)
        vg = jax.nn.gelu(vg).reshape(B, S, SGU_GROUPS, SGU_GROUP_DIM)
        out_a = chunked_sgu(u, vg, sgu_norm_g[l], sgu_norm_b[l],
                            sgu_w_s[l], sgu_b_s[l]).reshape(B, S, SGU_WIDTH)

        def heads(t):
            return t.reshape(B, S, SB_HEADS, SB_HEAD_DIM).transpose(0, 2, 1, 3)
        out_b = stick_breaking_attention(heads(q), heads(k), heads(vs))
        out_b = out_b.transpose(0, 2, 1, 3).reshape(B, S, SB_WIDTH)

        merged = jnp.concatenate([rmsnorm(out_a, sgu_out_g[l]),
                                  rmsnorm(out_b, sb_out_g[l])], axis=-1)
        h = h + rmsnorm(merged @ w_out[l], mix_post_g[l])

        c = memory_cross_attention(rmsnorm(h, xa_pre_g[l]), rmsnorm(mem, mem_norm_g[l]),
                                   xa_w_q[l], xa_w_kv[l], xa_w_o[l])
        h = h + rmsnorm(c, xa_post_g[l])

        f = swiglu(rmsnorm(h, ffn2_pre_g[l]), ffn2_w_gate[l], ffn2_w_up[l], ffn2_w_down[l])
        h = h + 0.5 * rmsnorm(f, ffn2_post_g[l])

        h = rmsnorm(h, final_norm_g[l])
    return h
```

```python
import functools

import jax
import jax.numpy as jnp
from jax import lax
from jax.experimental import pallas as pl
from jax.experimental.pallas import tpu as pltpu

D_MODEL = 1024
MEM_LEN = 256
SGU_WIDTH = 512
SGU_GROUPS = 4
SGU_GROUP_DIM = SGU_WIDTH // SGU_GROUPS
CHUNK = 128
SB_WIDTH = 512
SB_HEAD_DIM = 64
XA_HEADS = 4
XA_HEAD_DIM = D_MODEL // XA_HEADS
D_FF = 2816
IN_COLS = 2 * SGU_WIDTH + 3 * SB_WIDTH
EPS = 1e-6

LANES = 128
VMEM_LIMIT = 56 * 1024 * 1024

FFN_ROWS = 512
MIX_ROWS = 256
POST_ROWS = 256
SB_Q = 128
SB_K = 128

F32 = jnp.float32
BF16 = jnp.bfloat16


def _rms(x, g):
    return x * lax.rsqrt(jnp.mean(x * x, axis=-1, keepdims=True) + EPS) * g


def _dot(a, b):
    return jnp.dot(a, b, preferred_element_type=F32)


def _dot_nt(a, b):
    return lax.dot_general(a, b, (((1,), (1,)), ((), ())), preferred_element_type=F32)


def _rows(tm, width):
    return pl.BlockSpec((tm, width), lambda i: (i, 0))


def _resident(shape):
    return pl.BlockSpec(shape, lambda *_: (0,) * len(shape), pipeline_mode=pl.Buffered(1))


def _params(*semantics):
    return pltpu.CompilerParams(dimension_semantics=semantics, vmem_limit_bytes=VMEM_LIMIT)


def _ffn_kernel(h_ref, pre_ref, post_ref, wg_ref, wu_ref, wd_ref, *rest, final_norm):
    o_ref = rest[-1]
    h = h_ref[...]
    x = _rms(h, pre_ref[...]).astype(BF16)
    gate = _dot(x, wg_ref[...])
    up = _dot(x, wu_ref[...])
    act = (gate * jax.nn.sigmoid(gate) * up).astype(BF16)
    y = h + 0.5 * _rms(_dot(act, wd_ref[...]), post_ref[...])
    if final_norm:
        y = _rms(y, rest[0][...])
    o_ref[...] = y


def _ffn(h, pre_g, post_g, w_gate, w_up, w_down, final_g=None):
    t, d = h.shape
    f = w_gate.shape[1]
    vec = _resident((1, d))
    in_specs = [_rows(FFN_ROWS, d), vec, vec, _resident((d, f)), _resident((d, f)), _resident((f, d))]
    args = [h, pre_g, post_g, w_gate, w_up, w_down]
    if final_g is not None:
        in_specs.append(vec)
        args.append(final_g)
    return pl.pallas_call(
        functools.partial(_ffn_kernel, final_norm=final_g is not None),
        grid=(t // FFN_ROWS,),
        in_specs=in_specs,
        out_specs=_rows(FFN_ROWS, d),
        out_shape=jax.ShapeDtypeStruct((t, d), F32),
        compiler_params=_params("parallel"),
        name="ffn_final" if final_g is not None else "ffn",
    )(*args)


def _mix_in_kernel(h_ref, g_ref, win_ref, ng_ref, nb_ref, ws_ref, bs_ref, og_ref,
                   oa_ref, q_ref, k_ref, v_ref):
    tm = h_ref.shape[0]
    n = _rms(h_ref[...], g_ref[...]).astype(BF16)
    proj = _dot(n, win_ref[...])

    row = lax.broadcasted_iota(jnp.int32, (CHUNK, CHUNK), 0)
    col = lax.broadcasted_iota(jnp.int32, (CHUNK, CHUNK), 1)
    causal = col <= row

    groups = []
    ss = jnp.zeros((tm, 1), F32)
    for g in range(SGU_GROUPS):
        lo = g * SGU_GROUP_DIM
        u = jax.nn.gelu(proj[:, lo:lo + SGU_GROUP_DIM])
        vg = jax.nn.gelu(proj[:, SGU_WIDTH + lo:SGU_WIDTH + lo + SGU_GROUP_DIM])
        mu = jnp.mean(vg, axis=-1, keepdims=True)
        dv = vg - mu
        var = jnp.mean(dv * dv, axis=-1, keepdims=True)
        vn = (dv * lax.rsqrt(var + EPS) * ng_ref[g:g + 1, :] + nb_ref[g:g + 1, :]).astype(BF16)
        w = jnp.where(causal, ws_ref[g], 0.0).astype(BF16)
        mixed = jnp.concatenate(
            [_dot(w, vn[c * CHUNK:(c + 1) * CHUNK, :]) + bs_ref[g] for c in range(tm // CHUNK)],
            axis=0)
        oa = u * mixed
        ss = ss + jnp.sum(oa * oa, axis=-1, keepdims=True)
        groups.append(oa)
    inv = lax.rsqrt(ss * (1.0 / SGU_WIDTH) + EPS)
    for g in range(SGU_GROUPS):
        lo = g * SGU_GROUP_DIM
        oa_ref[:, lo:lo + SGU_GROUP_DIM] = (groups[g] * inv * og_ref[:, lo:lo + SGU_GROUP_DIM]).astype(BF16)

    base = 2 * SGU_WIDTH
    q_ref[...] = (proj[:, base:base + SB_WIDTH] * (SB_HEAD_DIM ** -0.5)).astype(BF16)
    k_ref[...] = proj[:, base + SB_WIDTH:base + 2 * SB_WIDTH].astype(BF16)
    v_ref[...] = proj[:, base + 2 * SB_WIDTH:base + 3 * SB_WIDTH].astype(BF16)


def _mix_in(h, pre_g, w_in, norm_g, norm_b, w_s, b_s, out_g):
    t, d = h.shape
    half = jax.ShapeDtypeStruct((t, SB_WIDTH), BF16)
    return pl.pallas_call(
        _mix_in_kernel,
        grid=(t // MIX_ROWS,),
        in_specs=[_rows(MIX_ROWS, d), _resident((1, d)), _resident((d, IN_COLS)),
                  _resident((SGU_GROUPS, SGU_GROUP_DIM)), _resident((SGU_GROUPS, SGU_GROUP_DIM)),
                  _resident((SGU_GROUPS, CHUNK, CHUNK)), _resident((SGU_GROUPS, CHUNK, 1)),
                  _resident((1, SGU_WIDTH))],
        out_specs=[_rows(MIX_ROWS, SGU_WIDTH)] + [_rows(MIX_ROWS, SB_WIDTH)] * 3,
        out_shape=[jax.ShapeDtypeStruct((t, SGU_WIDTH), BF16), half, half, half],
        compiler_params=_params("parallel"),
        name="mix_in",
    )(h, pre_g, w_in, norm_g, norm_b, w_s, b_s, out_g)


def _sb_kernel(q_ref, k_ref, v_ref, o_ref):
    i = pl.program_id(2)
    q = q_ref[...]
    lane = lax.broadcasted_iota(jnp.int32, (1, LANES), 1)
    first = lane < SB_HEAD_DIM
    zero = jnp.zeros_like(q)
    q_heads = (jnp.where(first, q, zero), jnp.where(first, zero, q))

    r = lax.broadcasted_iota(jnp.int32, (SB_K, SB_K), 0)
    c = lax.broadcasted_iota(jnp.int32, (SB_K, SB_K), 1)
    suffix = jnp.concatenate([(r >= c).astype(BF16), jnp.ones((SB_K, SB_K), BF16)], axis=1)
    suffix = jnp.concatenate([suffix, suffix], axis=0)
    strict = c < r

    def tile(j, carry, acc, diagonal):
        start = pl.multiple_of(j * SB_K, SB_K)
        kt = k_ref[pl.ds(start, SB_K), :]
        vt = v_ref[pl.ds(start, SB_K), :]
        weights, new_carry = [], []
        for qh, ch in zip(q_heads, carry):
            z = _dot_nt(qh, kt)
            sp = jnp.maximum(z, 0.0) + jnp.log1p(jnp.exp(-jnp.abs(z)))
            if diagonal:
                sp = jnp.where(strict, sp, 0.0)
            hi = sp.astype(BF16)
            lo = (sp - hi.astype(F32)).astype(BF16)
            sums = _dot(jnp.concatenate([hi, lo], axis=1), suffix)
            a = jnp.exp(z - (ch + sums[:, :SB_K]))
            if diagonal:
                a = jnp.where(strict, a, 0.0)
            weights.append(a.astype(BF16))
            new_carry.append(ch + sums[:, SB_K:])
        vzero = jnp.zeros_like(vt)
        v_heads = jnp.concatenate([jnp.where(first, vt, vzero), jnp.where(first, vzero, vt)], axis=0)
        acc = acc + _dot(jnp.concatenate(weights, axis=1), v_heads)
        return tuple(new_carry), acc

    zeros = jnp.zeros((SB_Q, LANES), F32)
    carry, acc = tile(i, (zeros, zeros), zeros, True)

    def body(n, state):
        carry, acc = state
        return tile(i - 1 - n, carry, acc, False)

    carry, acc = lax.fori_loop(0, i, body, (carry, acc))
    o_ref[...] = acc


def _sb_attention(q, k, v, batch, seq):
    t, w = q.shape
    nq = seq // SB_Q
    kv_spec = pl.BlockSpec((seq, LANES), lambda b, p, i: (b, p))
    return pl.pallas_call(
        _sb_kernel,
        grid=(batch, w // LANES, nq),
        in_specs=[pl.BlockSpec((SB_Q, LANES), lambda b, p, i: (b * nq + i, p)), kv_spec, kv_spec],
        out_specs=pl.BlockSpec((SB_Q, LANES), lambda b, p, i: (b * nq + i, p)),
        out_shape=jax.ShapeDtypeStruct((t, w), F32),
        compiler_params=_params("parallel", "parallel", "arbitrary"),
        name="sb_attn",
    )(q, k, v)


def _mem_kv_kernel(mem_ref, g_ref, w_ref, o_ref):
    o_ref[...] = _dot(_rms(mem_ref[...], g_ref[...]).astype(BF16), w_ref[...]).astype(BF16)


def _mem_kv(mem, g, w_kv):
    t, d = mem.shape
    n = w_kv.shape[1]
    return pl.pallas_call(
        _mem_kv_kernel,
        grid=(t // MEM_LEN,),
        in_specs=[_rows(MEM_LEN, d), _resident((1, d)), _resident((d, n))],
        out_specs=_rows(MEM_LEN, n),
        out_shape=jax.ShapeDtypeStruct((t, n), BF16),
        compiler_params=_params("parallel"),
        name="mem_kv",
    )(mem, g, w_kv)


def _post_mix_kernel(h_ref, oa_ref, ob_ref, sbg_ref, wout_ref, mpost_ref, xpre_ref, wq_ref,
                     kv_ref, wo_ref, xpost_ref, o_ref):
    ob = _rms(ob_ref[...], sbg_ref[...]).astype(BF16)
    merged = _dot(oa_ref[...], wout_ref[:SGU_WIDTH, :]) + _dot(ob, wout_ref[SGU_WIDTH:, :])
    h = h_ref[...] + _rms(merged, mpost_ref[...])

    x = _rms(h, xpre_ref[...]).astype(BF16)
    q = (_dot(x, wq_ref[...]) * (XA_HEAD_DIM ** -0.5)).astype(BF16)
    heads = []
    for hd in range(XA_HEADS):
        lo = hd * XA_HEAD_DIM
        logits = _dot_nt(q[:, lo:lo + XA_HEAD_DIM], kv_ref[:, lo:lo + XA_HEAD_DIM])
        e = jnp.exp(logits - jnp.max(logits, axis=-1, keepdims=True))
        o = _dot(e.astype(BF16), kv_ref[:, D_MODEL + lo:D_MODEL + lo + XA_HEAD_DIM])
        heads.append((o / jnp.sum(e, axis=-1, keepdims=True)).astype(BF16))
    c = _dot(jnp.concatenate(heads, axis=1), wo_ref[...])
    o_ref[...] = h + _rms(c, xpost_ref[...])


def _post_mix(h, out_a, out_b, sb_g, w_out, mix_post_g, xa_pre_g, w_q, kv, w_o, xa_post_g, seq):
    t, d = h.shape
    per_batch = seq // POST_ROWS
    vec = _resident((1, d))
    return pl.pallas_call(
        _post_mix_kernel,
        grid=(t // POST_ROWS,),
        in_specs=[_rows(POST_ROWS, d), _rows(POST_ROWS, SGU_WIDTH), _rows(POST_ROWS, SB_WIDTH),
                  _resident((1, SB_WIDTH)), _resident((d, d)), vec, vec, _resident((d, d)),
                  pl.BlockSpec((MEM_LEN, 2 * d), lambda i: (i // per_batch, 0)),
                  _resident((d, d)), vec],
        out_specs=_rows(POST_ROWS, d),
        out_shape=jax.ShapeDtypeStruct((t, d), F32),
        compiler_params=_params("parallel"),
        name="post_mix",
    )(h, out_a, out_b, sb_g, w_out, mix_post_g, xa_pre_g, w_q, kv, w_o, xa_post_g)


def kernel(x, mem, ffn1_pre_g, ffn1_post_g, ffn1_w_gate, ffn1_w_up, ffn1_w_down, mix_pre_g, mix_post_g, w_in, sgu_norm_g, sgu_norm_b, sgu_w_s, sgu_b_s, sgu_out_g, sb_out_g, w_out, xa_pre_g, xa_post_g, mem_norm_g, xa_w_q, xa_w_kv, xa_w_o, ffn2_pre_g, ffn2_post_g, ffn2_w_gate, ffn2_w_up, ffn2_w_down, final_norm_g):
    batch, seq, d = x.shape
    depth = ffn1_pre_g.shape[0]
    h = x.reshape(batch * seq, d)
    mem2 = mem.reshape(batch * mem.shape[1], d)

    def w16(w):
        return w.astype(BF16)

    for l in range(depth):
        h = _ffn(h, ffn1_pre_g[l:l + 1], ffn1_post_g[l:l + 1],
                 w16(ffn1_w_gate[l]), w16(ffn1_w_up[l]), w16(ffn1_w_down[l]))
        out_a, q, k, v = _mix_in(h, mix_pre_g[l:l + 1], w16(w_in[l]), sgu_norm_g[l], sgu_norm_b[l],
                                 sgu_w_s[l], sgu_b_s[l][:, :, None], sgu_out_g[l:l + 1])
        out_b = _sb_attention(q, k, v, batch, seq)
        kv = _mem_kv(mem2, mem_norm_g[l:l + 1], w16(xa_w_kv[l]))
        h = _post_mix(h, out_a, out_b, sb_out_g[l:l + 1], w16(w_out[l]), mix_post_g[l:l + 1],
                      xa_pre_g[l:l + 1], w16(xa_w_q[l]), kv, w16(xa_w_o[l]), xa_post_g[l:l + 1], seq)
        h = _ffn(h, ffn2_pre_g[l:l + 1], ffn2_post_g[l:l + 1],
                 w16(ffn2_w_gate[l]), w16(ffn2_w_up[l]), w16(ffn2_w_down[l]),
                 final_g=final_norm_g[l:l + 1])
    return h.reshape(batch, seq, d)
```

```python
import functools

import jax
import jax.numpy as jnp
from jax import lax
from jax.experimental import pallas as pl
from jax.experimental.pallas import tpu as pltpu

D_MODEL = 1024
MEM_LEN = 256
SGU_WIDTH = 512
SGU_GROUPS = 4
SGU_GROUP_DIM = SGU_WIDTH // SGU_GROUPS
CHUNK = 128
SB_WIDTH = 512
SB_HEAD_DIM = 64
XA_HEADS = 4
XA_HEAD_DIM = D_MODEL // XA_HEADS
D_FF = 2816
IN_COLS = 2 * SGU_WIDTH + 3 * SB_WIDTH
EPS = 1e-6

LANES = 128
VMEM_LIMIT = 56 * 1024 * 1024

FFN_ROWS = 512
MIX_ROWS = 256
POST_ROWS = 256
SB_TILE = 128
SB_SUBS = 4
SB_WINDOW = 3
SB_DONE = 105.0

F32 = jnp.float32
BF16 = jnp.bfloat16


def _rms(x, g):
    return x * lax.rsqrt(jnp.mean(x * x, axis=-1, keepdims=True) + EPS) * g


def _dot(a, b):
    return jnp.dot(a, b, preferred_element_type=F32)


def _dot_nt(a, b):
    return lax.dot_general(a, b, (((1,), (1,)), ((), ())), preferred_element_type=F32)


def _rows(tm, width):
    return pl.BlockSpec((tm, width), lambda i: (i, 0))


def _resident(shape):
    return pl.BlockSpec(shape, lambda *_: (0,) * len(shape), pipeline_mode=pl.Buffered(1))


def _params(*semantics):
    return pltpu.CompilerParams(dimension_semantics=semantics, vmem_limit_bytes=VMEM_LIMIT)


def _ffn_kernel(h_ref, pre_ref, post_ref, wg_ref, wu_ref, wd_ref, *rest, final_norm):
    o_ref = rest[-1]
    h = h_ref[...]
    x = _rms(h, pre_ref[...]).astype(BF16)
    gate = _dot(x, wg_ref[...])
    up = _dot(x, wu_ref[...])
    act = (gate * jax.nn.sigmoid(gate) * up).astype(BF16)
    y = h + 0.5 * _rms(_dot(act, wd_ref[...]), post_ref[...])
    if final_norm:
        y = _rms(y, rest[0][...])
    o_ref[...] = y


def _ffn(h, pre_g, post_g, w_gate, w_up, w_down, final_g=None):
    t, d = h.shape
    f = w_gate.shape[1]
    vec = _resident((1, d))
    in_specs = [_rows(FFN_ROWS, d), vec, vec, _resident((d, f)), _resident((d, f)), _resident((f, d))]
    args = [h, pre_g, post_g, w_gate, w_up, w_down]
    if final_g is not None:
        in_specs.append(vec)
        args.append(final_g)
    return pl.pallas_call(
        functools.partial(_ffn_kernel, final_norm=final_g is not None),
        grid=(t // FFN_ROWS,),
        in_specs=in_specs,
        out_specs=_rows(FFN_ROWS, d),
        out_shape=jax.ShapeDtypeStruct((t, d), F32),
        compiler_params=_params("parallel"),
        name="ffn_final" if final_g is not None else "ffn",
    )(*args)


def _mix_in_kernel(h_ref, g_ref, win_ref, ng_ref, nb_ref, ws_ref, bs_ref, og_ref,
                   oa_ref, q_ref, k_ref, v_ref):
    tm = h_ref.shape[0]
    n = _rms(h_ref[...], g_ref[...]).astype(BF16)
    proj = _dot(n, win_ref[...])

    row = lax.broadcasted_iota(jnp.int32, (CHUNK, CHUNK), 0)
    col = lax.broadcasted_iota(jnp.int32, (CHUNK, CHUNK), 1)
    causal = col <= row

    groups = []
    ss = jnp.zeros((tm, 1), F32)
    for g in range(SGU_GROUPS):
        lo = g * SGU_GROUP_DIM
        u = jax.nn.gelu(proj[:, lo:lo + SGU_GROUP_DIM])
        vg = jax.nn.gelu(proj[:, SGU_WIDTH + lo:SGU_WIDTH + lo + SGU_GROUP_DIM])
        mu = jnp.mean(vg, axis=-1, keepdims=True)
        dv = vg - mu
        var = jnp.mean(dv * dv, axis=-1, keepdims=True)
        vn = (dv * lax.rsqrt(var + EPS) * ng_ref[g:g + 1, :] + nb_ref[g:g + 1, :]).astype(BF16)
        w = jnp.where(causal, ws_ref[g], 0.0).astype(BF16)
        mixed = jnp.concatenate(
            [_dot(w, vn[c * CHUNK:(c + 1) * CHUNK, :]) + bs_ref[g] for c in range(tm // CHUNK)],
            axis=0)
        oa = u * mixed
        ss = ss + jnp.sum(oa * oa, axis=-1, keepdims=True)
        groups.append(oa)
    inv = lax.rsqrt(ss * (1.0 / SGU_WIDTH) + EPS)
    for g in range(SGU_GROUPS):
        lo = g * SGU_GROUP_DIM
        oa_ref[:, lo:lo + SGU_GROUP_DIM] = (groups[g] * inv * og_ref[:, lo:lo + SGU_GROUP_DIM]).astype(BF16)

    base = 2 * SGU_WIDTH
    q_ref[...] = (proj[:, base:base + SB_WIDTH] * (SB_HEAD_DIM ** -0.5)).astype(BF16)
    k_ref[...] = proj[:, base + SB_WIDTH:base + 2 * SB_WIDTH].astype(BF16)
    v_ref[...] = proj[:, base + 2 * SB_WIDTH:base + 3 * SB_WIDTH].astype(BF16)


def _mix_in(h, pre_g, w_in, norm_g, norm_b, w_s, b_s, out_g):
    t, d = h.shape
    half = jax.ShapeDtypeStruct((t, SB_WIDTH), BF16)
    return pl.pallas_call(
        _mix_in_kernel,
        grid=(t // MIX_ROWS,),
        in_specs=[_rows(MIX_ROWS, d), _resident((1, d)), _resident((d, IN_COLS)),
                  _resident((SGU_GROUPS, SGU_GROUP_DIM)), _resident((SGU_GROUPS, SGU_GROUP_DIM)),
                  _resident((SGU_GROUPS, CHUNK, CHUNK)), _resident((SGU_GROUPS, CHUNK, 1)),
                  _resident((1, SGU_WIDTH))],
        out_specs=[_rows(MIX_ROWS, SGU_WIDTH)] + [_rows(MIX_ROWS, SB_WIDTH)] * 3,
        out_shape=[jax.ShapeDtypeStruct((t, SGU_WIDTH), BF16), half, half, half],
        compiler_params=_params("parallel"),
        name="mix_in",
    )(h, pre_g, w_in, norm_g, norm_b, w_s, b_s, out_g)


def _sb_kernel(q_ref, k_ref, v_ref, o_ref, carry_ref, acc_ref):
    i = pl.program_id(2)
    lane = lax.broadcasted_iota(jnp.int32, (1, LANES), 1)
    first = lane < SB_HEAD_DIM

    r = lax.broadcasted_iota(jnp.int32, (SB_TILE, SB_TILE), 0)
    c = lax.broadcasted_iota(jnp.int32, (SB_TILE, SB_TILE), 1)
    suffix = jnp.concatenate([(r >= c).astype(BF16), jnp.ones((SB_TILE, SB_TILE), BF16)], axis=1)
    suffix = jnp.concatenate([suffix, suffix], axis=0)
    strict = c < r

    def q_heads(sub):
        q = q_ref[sub * SB_TILE:(sub + 1) * SB_TILE, :]
        zero = jnp.zeros_like(q)
        return jnp.where(first, q, zero), jnp.where(first, zero, q)

    def window(qs, start, n, carry, acc, diagonal):
        kw = k_ref[pl.ds(start, n * SB_TILE), :]
        vw = v_ref[pl.ds(start, n * SB_TILE), :]
        weights, new_carry = [], []
        for qh, ch in zip(qs, carry):
            z = _dot_nt(qh, kw)
            parts = [None] * n
            for t in reversed(range(n)):
                zt = z[:, t * SB_TILE:(t + 1) * SB_TILE]
                masked = diagonal and t == n - 1
                sp = jnp.maximum(zt, 0.0) + jnp.log(1.0 + jnp.exp(-jnp.abs(zt)))
                if masked:
                    sp = jnp.where(strict, sp, 0.0)
                hi = sp.astype(BF16)
                lo = (sp - hi.astype(F32)).astype(BF16)
                sums = _dot(jnp.concatenate([hi, lo], axis=1), suffix)
                a = jnp.exp(zt - (ch + sums[:, :SB_TILE]))
                if masked:
                    a = jnp.where(strict, a, 0.0)
                parts[t] = a.astype(BF16)
                ch = ch + sums[:, SB_TILE:]
            weights.append(jnp.concatenate(parts, axis=1))
            new_carry.append(ch)
        vzero = jnp.zeros_like(vw)
        v_heads = jnp.concatenate([jnp.where(first, vw, vzero), jnp.where(first, vzero, vw)], axis=0)
        acc = acc + _dot(jnp.concatenate(weights, axis=1), v_heads)
        return new_carry, acc

    def save(sub, carry, acc):
        carry_ref[sub, 0] = carry[0]
        carry_ref[sub, 1] = carry[1]
        acc_ref[sub] = acc

    def first_step(sub, n, diag_tile):
        zeros = jnp.zeros((SB_TILE, LANES), F32)
        start = (diag_tile - (n - 1)) * SB_TILE
        if not isinstance(start, int):
            start = pl.multiple_of(start, SB_TILE)
        save(sub, *window(q_heads(sub), start, n, (zeros, zeros), zeros, True))

    @pl.when(i == 0)
    def _():
        for sub in range(SB_SUBS):
            first_step(sub, min(sub + 1, SB_WINDOW), sub)

    @pl.when(i > 0)
    def _():
        for sub in range(SB_SUBS):
            first_step(sub, SB_WINDOW, i * SB_SUBS + sub)

    lowest = carry_ref[0, 0]
    for sub in range(SB_SUBS):
        for hd in range(2):
            lowest = jnp.minimum(lowest, carry_ref[sub, hd])

    @pl.when(jnp.min(lowest) < SB_DONE)
    def _():
        for sub in range(SB_SUBS):
            diag_tile = i * SB_SUBS + sub
            qs = q_heads(sub)

            def cond(state):
                j, lowest_carry = state
                return jnp.logical_and(j >= 0, lowest_carry < SB_DONE)

            def body(state, sub=sub, qs=qs):
                j, _ = state
                carry, acc = window(qs, pl.multiple_of(j * SB_TILE, SB_TILE), 1,
                                    (carry_ref[sub, 0], carry_ref[sub, 1]), acc_ref[sub], False)
                save(sub, carry, acc)
                return j - 1, jnp.min(jnp.minimum(carry[0], carry[1]))

            done_tiles = jnp.minimum(diag_tile + 1, SB_WINDOW)
            lax.while_loop(cond, body, (diag_tile - done_tiles,
                                        jnp.min(jnp.minimum(carry_ref[sub, 0], carry_ref[sub, 1]))))

    for sub in range(SB_SUBS):
        o_ref[sub * SB_TILE:(sub + 1) * SB_TILE, :] = acc_ref[sub]


def _sb_attention(q, k, v, batch, seq):
    t, w = q.shape
    rows = SB_SUBS * SB_TILE
    nq = seq // rows
    kv_spec = pl.BlockSpec((seq, LANES), lambda b, p, i: (b, p))
    return pl.pallas_call(
        _sb_kernel,
        grid=(batch, w // LANES, nq),
        in_specs=[pl.BlockSpec((rows, LANES), lambda b, p, i: (b * nq + i, p)), kv_spec, kv_spec],
        out_specs=pl.BlockSpec((rows, LANES), lambda b, p, i: (b * nq + i, p)),
        out_shape=jax.ShapeDtypeStruct((t, w), F32),
        scratch_shapes=[pltpu.VMEM((SB_SUBS, 2, SB_TILE, LANES), F32),
                        pltpu.VMEM((SB_SUBS, SB_TILE, LANES), F32)],
        compiler_params=_params("parallel", "parallel", "arbitrary"),
        name="sb_attn",
    )(q, k, v)


def _mem_kv_kernel(mem_ref, g_ref, w_ref, o_ref):
    o_ref[...] = _dot(_rms(mem_ref[...], g_ref[...]).astype(BF16), w_ref[...]).astype(BF16)


def _mem_kv(mem, g, w_kv):
    t, d = mem.shape
    n = w_kv.shape[1]
    return pl.pallas_call(
        _mem_kv_kernel,
        grid=(t // MEM_LEN,),
        in_specs=[_rows(MEM_LEN, d), _resident((1, d)), _resident((d, n))],
        out_specs=_rows(MEM_LEN, n),
        out_shape=jax.ShapeDtypeStruct((t, n), BF16),
        compiler_params=_params("parallel"),
        name="mem_kv",
    )(mem, g, w_kv)


def _post_mix_kernel(h_ref, oa_ref, ob_ref, sbg_ref, wout_ref, mpost_ref, xpre_ref, wq_ref,
                     kv_ref, wo_ref, xpost_ref, o_ref):
    ob = _rms(ob_ref[...], sbg_ref[...]).astype(BF16)
    merged = _dot(oa_ref[...], wout_ref[:SGU_WIDTH, :]) + _dot(ob, wout_ref[SGU_WIDTH:, :])
    h = h_ref[...] + _rms(merged, mpost_ref[...])

    x = _rms(h, xpre_ref[...]).astype(BF16)
    q = (_dot(x, wq_ref[...]) * (XA_HEAD_DIM ** -0.5)).astype(BF16)
    heads = []
    for hd in range(XA_HEADS):
        lo = hd * XA_HEAD_DIM
        logits = _dot_nt(q[:, lo:lo + XA_HEAD_DIM], kv_ref[:, lo:lo + XA_HEAD_DIM])
        e = jnp.exp(logits - jnp.max(logits, axis=-1, keepdims=True))
        o = _dot(e.astype(BF16), kv_ref[:, D_MODEL + lo:D_MODEL + lo + XA_HEAD_DIM])
        heads.append((o / jnp.sum(e, axis=-1, keepdims=True)).astype(BF16))
    c = _dot(jnp.concatenate(heads, axis=1), wo_ref[...])
    o_ref[...] = h + _rms(c, xpost_ref[...])


def _post_mix(h, out_a, out_b, sb_g, w_out, mix_post_g, xa_pre_g, w_q, kv, w_o, xa_post_g, seq):
    t, d = h.shape
    per_batch = seq // POST_ROWS
    vec = _resident((1, d))
    return pl.pallas_call(
        _post_mix_kernel,
        grid=(t // POST_ROWS,),
        in_specs=[_rows(POST_ROWS, d), _rows(POST_ROWS, SGU_WIDTH), _rows(POST_ROWS, SB_WIDTH),
                  _resident((1, SB_WIDTH)), _resident((d, d)), vec, vec, _resident((d, d)),
                  pl.BlockSpec((MEM_LEN, 2 * d), lambda i: (i // per_batch, 0)),
                  _resident((d, d)), vec],
        out_specs=_rows(POST_ROWS, d),
        out_shape=jax.ShapeDtypeStruct((t, d), F32),
        compiler_params=_params("parallel"),
        name="post_mix",
    )(h, out_a, out_b, sb_g, w_out, mix_post_g, xa_pre_g, w_q, kv, w_o, xa_post_g)


def kernel(x, mem, ffn1_pre_g, ffn1_post_g, ffn1_w_gate, ffn1_w_up, ffn1_w_down, mix_pre_g, mix_post_g, w_in, sgu_norm_g, sgu_norm_b, sgu_w_s, sgu_b_s, sgu_out_g, sb_out_g, w_out, xa_pre_g, xa_post_g, mem_norm_g, xa_w_q, xa_w_kv, xa_w_o, ffn2_pre_g, ffn2_post_g, ffn2_w_gate, ffn2_w_up, ffn2_w_down, final_norm_g):
    batch, seq, d = x.shape
    depth = ffn1_pre_g.shape[0]
    h = x.reshape(batch * seq, d)
    mem2 = mem.reshape(batch * mem.shape[1], d)

    def w16(w):
        return w.astype(BF16)

    for l in range(depth):
        h = _ffn(h, ffn1_pre_g[l:l + 1], ffn1_post_g[l:l + 1],
                 w16(ffn1_w_gate[l]), w16(ffn1_w_up[l]), w16(ffn1_w_down[l]))
        out_a, q, k, v = _mix_in(h, mix_pre_g[l:l + 1], w16(w_in[l]), sgu_norm_g[l], sgu_norm_b[l],
                                 sgu_w_s[l], sgu_b_s[l][:, :, None], sgu_out_g[l:l + 1])
        out_b = _sb_attention(q, k, v, batch, seq)
        kv = _mem_kv(mem2, mem_norm_g[l:l + 1], w16(xa_w_kv[l]))
        h = _post_mix(h, out_a, out_b, sb_out_g[l:l + 1], w16(w_out[l]), mix_post_g[l:l + 1],
                      xa_pre_g[l:l + 1], w16(xa_w_q[l]), kv, w16(xa_w_o[l]), xa_post_g[l:l + 1], seq)
        h = _ffn(h, ffn2_pre_g[l:l + 1], ffn2_post_g[l:l + 1],
                 w16(ffn2_w_gate[l]), w16(ffn2_w_up[l]), w16(ffn2_w_down[l]),
                 final_g=final_norm_g[l:l + 1])
    return h.reshape(batch, seq, d)
```

```python
import functools

import jax
import jax.numpy as jnp
from jax import lax
from jax.experimental import pallas as pl
from jax.experimental.pallas import tpu as pltpu

D_MODEL = 1024
MEM_LEN = 256
SGU_WIDTH = 512
SGU_GROUPS = 4
SGU_GROUP_DIM = SGU_WIDTH // SGU_GROUPS
CHUNK = 128
SB_WIDTH = 512
SB_HEAD_DIM = 64
XA_HEADS = 4
XA_HEAD_DIM = D_MODEL // XA_HEADS
D_FF = 2816
IN_COLS = 2 * SGU_WIDTH + 3 * SB_WIDTH
EPS = 1e-6

LANES = 128
VMEM_LIMIT = 56 * 1024 * 1024

FFN_ROWS = 512
MIX_ROWS = 256
POST_ROWS = 256
SB_TILE = 128
SB_SUBS = 8
SB_WINDOW = 3
SB_DONE = 105.0
SB_CLAMP = 80.0

F32 = jnp.float32
BF16 = jnp.bfloat16


def _rms(x, g):
    return x * lax.rsqrt(jnp.mean(x * x, axis=-1, keepdims=True) + EPS) * g


def _dot(a, b):
    return jnp.dot(a, b, preferred_element_type=F32)


def _dot_nt(a, b):
    return lax.dot_general(a, b, (((1,), (1,)), ((), ())), preferred_element_type=F32)


def _rows(tm, width):
    return pl.BlockSpec((tm, width), lambda i: (i, 0))


def _resident(shape):
    return pl.BlockSpec(shape, lambda *_: (0,) * len(shape), pipeline_mode=pl.Buffered(1))


def _params(*semantics):
    return pltpu.CompilerParams(dimension_semantics=semantics, vmem_limit_bytes=VMEM_LIMIT)


def _ffn_kernel(h_ref, pre_ref, post_ref, wg_ref, wu_ref, wd_ref, *rest, final_norm):
    o_ref = rest[-1]
    h = h_ref[...]
    x = _rms(h, pre_ref[...]).astype(BF16)
    gate = _dot(x, wg_ref[...])
    up = _dot(x, wu_ref[...])
    act = (gate * jax.nn.sigmoid(gate) * up).astype(BF16)
    y = h + 0.5 * _rms(_dot(act, wd_ref[...]), post_ref[...])
    if final_norm:
        y = _rms(y, rest[0][...])
    o_ref[...] = y


def _ffn(h, pre_g, post_g, w_gate, w_up, w_down, final_g=None):
    t, d = h.shape
    f = w_gate.shape[1]
    vec = _resident((1, d))
    in_specs = [_rows(FFN_ROWS, d), vec, vec, _resident((d, f)), _resident((d, f)), _resident((f, d))]
    args = [h, pre_g, post_g, w_gate, w_up, w_down]
    if final_g is not None:
        in_specs.append(vec)
        args.append(final_g)
    return pl.pallas_call(
        functools.partial(_ffn_kernel, final_norm=final_g is not None),
        grid=(t // FFN_ROWS,),
        in_specs=in_specs,
        out_specs=_rows(FFN_ROWS, d),
        out_shape=jax.ShapeDtypeStruct((t, d), F32),
        compiler_params=_params("parallel"),
        name="ffn_final" if final_g is not None else "ffn",
    )(*args)


def _mix_in_kernel(h_ref, g_ref, win_ref, ng_ref, nb_ref, ws_ref, bs_ref, og_ref,
                   oa_ref, q_ref, k_ref, v_ref):
    tm = h_ref.shape[0]
    n = _rms(h_ref[...], g_ref[...]).astype(BF16)
    proj = _dot(n, win_ref[...])

    row = lax.broadcasted_iota(jnp.int32, (CHUNK, CHUNK), 0)
    col = lax.broadcasted_iota(jnp.int32, (CHUNK, CHUNK), 1)
    causal = col <= row

    groups = []
    ss = jnp.zeros((tm, 1), F32)
    for g in range(SGU_GROUPS):
        lo = g * SGU_GROUP_DIM
        u = jax.nn.gelu(proj[:, lo:lo + SGU_GROUP_DIM])
        vg = jax.nn.gelu(proj[:, SGU_WIDTH + lo:SGU_WIDTH + lo + SGU_GROUP_DIM])
        mu = jnp.mean(vg, axis=-1, keepdims=True)
        dv = vg - mu
        var = jnp.mean(dv * dv, axis=-1, keepdims=True)
        vn = (dv * lax.rsqrt(var + EPS) * ng_ref[g:g + 1, :] + nb_ref[g:g + 1, :]).astype(BF16)
        w = jnp.where(causal, ws_ref[g], 0.0).astype(BF16)
        mixed = jnp.concatenate(
            [_dot(w, vn[c * CHUNK:(c + 1) * CHUNK, :]) + bs_ref[g] for c in range(tm // CHUNK)],
            axis=0)
        oa = u * mixed
        ss = ss + jnp.sum(oa * oa, axis=-1, keepdims=True)
        groups.append(oa)
    inv = lax.rsqrt(ss * (1.0 / SGU_WIDTH) + EPS)
    for g in range(SGU_GROUPS):
        lo = g * SGU_GROUP_DIM
        oa_ref[:, lo:lo + SGU_GROUP_DIM] = (groups[g] * inv * og_ref[:, lo:lo + SGU_GROUP_DIM]).astype(BF16)

    base = 2 * SGU_WIDTH
    q_ref[...] = (proj[:, base:base + SB_WIDTH] * (SB_HEAD_DIM ** -0.5)).astype(BF16)
    k_ref[...] = proj[:, base + SB_WIDTH:base + 2 * SB_WIDTH].astype(BF16)
    v_ref[...] = proj[:, base + 2 * SB_WIDTH:base + 3 * SB_WIDTH].astype(BF16)


def _mix_in(h, pre_g, w_in, norm_g, norm_b, w_s, b_s, out_g):
    t, d = h.shape
    half = jax.ShapeDtypeStruct((t, SB_WIDTH), BF16)
    return pl.pallas_call(
        _mix_in_kernel,
        grid=(t // MIX_ROWS,),
        in_specs=[_rows(MIX_ROWS, d), _resident((1, d)), _resident((d, IN_COLS)),
                  _resident((SGU_GROUPS, SGU_GROUP_DIM)), _resident((SGU_GROUPS, SGU_GROUP_DIM)),
                  _resident((SGU_GROUPS, CHUNK, CHUNK)), _resident((SGU_GROUPS, CHUNK, 1)),
                  _resident((1, SGU_WIDTH))],
        out_specs=[_rows(MIX_ROWS, SGU_WIDTH)] + [_rows(MIX_ROWS, SB_WIDTH)] * 3,
        out_shape=[jax.ShapeDtypeStruct((t, SGU_WIDTH), BF16), half, half, half],
        compiler_params=_params("parallel"),
        name="mix_in",
    )(h, pre_g, w_in, norm_g, norm_b, w_s, b_s, out_g)


def _sb_kernel(q_ref, k_ref, v_ref, o_ref, carry_ref, acc_ref):
    i = pl.program_id(2)
    lane = lax.broadcasted_iota(jnp.int32, (1, LANES), 1)
    first = lane < SB_HEAD_DIM

    r = lax.broadcasted_iota(jnp.int32, (SB_TILE, SB_TILE), 0)
    c = lax.broadcasted_iota(jnp.int32, (SB_TILE, SB_TILE), 1)
    suffix = jnp.concatenate([(r >= c).astype(BF16), jnp.ones((SB_TILE, SB_TILE), BF16)], axis=1)
    suffix = jnp.concatenate([suffix, suffix], axis=0)
    strict = c < r

    def q_heads(sub):
        q = q_ref[sub * SB_TILE:(sub + 1) * SB_TILE, :]
        zero = jnp.zeros_like(q)
        return jnp.where(first, q, zero), jnp.where(first, zero, q)

    def scores(qs, start, n, diagonal):
        kw = k_ref[pl.ds(start, n * SB_TILE), :]
        zs, split = [], []
        for qh in qs:
            z = _dot_nt(qh, kw)
            for t in range(n):
                zt = z[:, t * SB_TILE:(t + 1) * SB_TILE]
                sp = jnp.maximum(zt, jnp.log(1.0 + jnp.exp(jnp.minimum(zt, SB_CLAMP))))
                if diagonal and t == n - 1:
                    sp = jnp.where(strict, sp, 0.0)
                hi = sp.astype(BF16)
                lo = (sp - hi.astype(F32)).astype(BF16)
                zs.append(zt)
                split.append(jnp.concatenate([hi, lo], axis=1))
        return zs, jnp.concatenate(split, axis=0)

    def fold(zs, sums, start, n, carry, acc, diagonal):
        vw = v_ref[pl.ds(start, n * SB_TILE), :]
        weights, new_carry = [], []
        for hd, ch in enumerate(carry):
            parts = [None] * n
            for t in reversed(range(n)):
                blk = sums[(hd * n + t) * SB_TILE:(hd * n + t + 1) * SB_TILE]
                a = jnp.exp(zs[hd * n + t] - (ch + blk[:, :SB_TILE]))
                if diagonal and t == n - 1:
                    a = jnp.where(strict, a, 0.0)
                parts[t] = a.astype(BF16)
                ch = ch + blk[:, SB_TILE:]
            weights.append(jnp.concatenate(parts, axis=1))
            new_carry.append(ch)
        vzero = jnp.zeros_like(vw)
        v_heads = jnp.concatenate([jnp.where(first, vw, vzero), jnp.where(first, vzero, vw)], axis=0)
        return new_carry, acc + _dot(jnp.concatenate(weights, axis=1), v_heads)

    def save(sub, carry, acc):
        carry_ref[sub, 0] = carry[0]
        carry_ref[sub, 1] = carry[1]
        acc_ref[sub] = acc

    def first_windows(windows):
        zeros = jnp.zeros((SB_TILE, LANES), F32)
        staged = [scores(q_heads(sub), start, n, True) for sub, n, start in windows]
        sums = [_dot(split, suffix) for _, split in staged]
        for (sub, n, start), (zs, _), sm in zip(windows, staged, sums):
            save(sub, *fold(zs, sm, start, n, (zeros, zeros), zeros, True))

    @pl.when(i == 0)
    def _():
        first_windows([(sub, min(sub + 1, SB_WINDOW), max(sub + 1 - SB_WINDOW, 0) * SB_TILE)
                       for sub in range(SB_SUBS)])

    @pl.when(i > 0)
    def _():
        first_windows([(sub, SB_WINDOW,
                        pl.multiple_of((i * SB_SUBS + sub - (SB_WINDOW - 1)) * SB_TILE, SB_TILE))
                       for sub in range(SB_SUBS)])

    lowest = carry_ref[0, 0]
    for sub in range(SB_SUBS):
        for hd in range(2):
            lowest = jnp.minimum(lowest, carry_ref[sub, hd])

    @pl.when(jnp.min(lowest) < SB_DONE)
    def _():
        for sub in range(SB_SUBS):
            diag_tile = i * SB_SUBS + sub
            qs = q_heads(sub)

            def cond(state):
                j, lowest_carry = state
                return jnp.logical_and(j >= 0, lowest_carry < SB_DONE)

            def body(state, sub=sub, qs=qs):
                j, _ = state
                start = pl.multiple_of(j * SB_TILE, SB_TILE)
                zs, split = scores(qs, start, 1, False)
                carry, acc = fold(zs, _dot(split, suffix), start, 1,
                                  (carry_ref[sub, 0], carry_ref[sub, 1]), acc_ref[sub], False)
                save(sub, carry, acc)
                return j - 1, jnp.min(jnp.minimum(carry[0], carry[1]))

            done_tiles = jnp.minimum(diag_tile + 1, SB_WINDOW)
            lax.while_loop(cond, body, (diag_tile - done_tiles,
                                        jnp.min(jnp.minimum(carry_ref[sub, 0], carry_ref[sub, 1]))))

    for sub in range(SB_SUBS):
        o_ref[sub * SB_TILE:(sub + 1) * SB_TILE, :] = acc_ref[sub]


def _sb_attention(q, k, v, batch, seq):
    t, w = q.shape
    rows = SB_SUBS * SB_TILE
    nq = seq // rows
    kv_spec = pl.BlockSpec((seq, LANES), lambda b, p, i: (b, p))
    return pl.pallas_call(
        _sb_kernel,
        grid=(batch, w // LANES, nq),
        in_specs=[pl.BlockSpec((rows, LANES), lambda b, p, i: (b * nq + i, p)), kv_spec, kv_spec],
        out_specs=pl.BlockSpec((rows, LANES), lambda b, p, i: (b * nq + i, p)),
        out_shape=jax.ShapeDtypeStruct((t, w), F32),
        scratch_shapes=[pltpu.VMEM((SB_SUBS, 2, SB_TILE, LANES), F32),
                        pltpu.VMEM((SB_SUBS, SB_TILE, LANES), F32)],
        compiler_params=_params("parallel", "parallel", "arbitrary"),
        name="sb_attn",
    )(q, k, v)


def _mem_kv_kernel(mem_ref, g_ref, w_ref, o_ref):
    o_ref[...] = _dot(_rms(mem_ref[...], g_ref[...]).astype(BF16), w_ref[...]).astype(BF16)


def _mem_kv(mem, g, w_kv):
    t, d = mem.shape
    n = w_kv.shape[1]
    return pl.pallas_call(
        _mem_kv_kernel,
        grid=(t // MEM_LEN,),
        in_specs=[_rows(MEM_LEN, d), _resident((1, d)), _resident((d, n))],
        out_specs=_rows(MEM_LEN, n),
        out_shape=jax.ShapeDtypeStruct((t, n), BF16),
        compiler_params=_params("parallel"),
        name="mem_kv",
    )(mem, g, w_kv)


def _post_mix_kernel(h_ref, oa_ref, ob_ref, sbg_ref, wout_ref, mpost_ref, xpre_ref, wq_ref,
                     kv_ref, wo_ref, xpost_ref, o_ref):
    ob = _rms(ob_ref[...], sbg_ref[...]).astype(BF16)
    merged = _dot(oa_ref[...], wout_ref[:SGU_WIDTH, :]) + _dot(ob, wout_ref[SGU_WIDTH:, :])
    h = h_ref[...] + _rms(merged, mpost_ref[...])

    x = _rms(h, xpre_ref[...]).astype(BF16)
    q = (_dot(x, wq_ref[...]) * (XA_HEAD_DIM ** -0.5)).astype(BF16)
    heads = []
    for hd in range(XA_HEADS):
        lo = hd * XA_HEAD_DIM
        logits = _dot_nt(q[:, lo:lo + XA_HEAD_DIM], kv_ref[:, lo:lo + XA_HEAD_DIM])
        e = jnp.exp(logits - jnp.max(logits, axis=-1, keepdims=True))
        o = _dot(e.astype(BF16), kv_ref[:, D_MODEL + lo:D_MODEL + lo + XA_HEAD_DIM])
        heads.append((o / jnp.sum(e, axis=-1, keepdims=True)).astype(BF16))
    c = _dot(jnp.concatenate(heads, axis=1), wo_ref[...])
    o_ref[...] = h + _rms(c, xpost_ref[...])


def _post_mix(h, out_a, out_b, sb_g, w_out, mix_post_g, xa_pre_g, w_q, kv, w_o, xa_post_g, seq):
    t, d = h.shape
    per_batch = seq // POST_ROWS
    vec = _resident((1, d))
    return pl.pallas_call(
        _post_mix_kernel,
        grid=(t // POST_ROWS,),
        in_specs=[_rows(POST_ROWS, d), _rows(POST_ROWS, SGU_WIDTH), _rows(POST_ROWS, SB_WIDTH),
                  _resident((1, SB_WIDTH)), _resident((d, d)), vec, vec, _resident((d, d)),
                  pl.BlockSpec((MEM_LEN, 2 * d), lambda i: (i // per_batch, 0)),
                  _resident((d, d)), vec],
        out_specs=_rows(POST_ROWS, d),
        out_shape=jax.ShapeDtypeStruct((t, d), F32),
        compiler_params=_params("parallel"),
        name="post_mix",
    )(h, out_a, out_b, sb_g, w_out, mix_post_g, xa_pre_g, w_q, kv, w_o, xa_post_g)


def kernel(x, mem, ffn1_pre_g, ffn1_post_g, ffn1_w_gate, ffn1_w_up, ffn1_w_down, mix_pre_g, mix_post_g, w_in, sgu_norm_g, sgu_norm_b, sgu_w_s, sgu_b_s, sgu_out_g, sb_out_g, w_out, xa_pre_g, xa_post_g, mem_norm_g, xa_w_q, xa_w_kv, xa_w_o, ffn2_pre_g, ffn2_post_g, ffn2_w_gate, ffn2_w_up, ffn2_w_down, final_norm_g):
    batch, seq, d = x.shape
    depth = ffn1_pre_g.shape[0]
    h = x.reshape(batch * seq, d)
    mem2 = mem.reshape(batch * mem.shape[1], d)

    def w16(w):
        return w.astype(BF16)

    for l in range(depth):
        h = _ffn(h, ffn1_pre_g[l:l + 1], ffn1_post_g[l:l + 1],
                 w16(ffn1_w_gate[l]), w16(ffn1_w_up[l]), w16(ffn1_w_down[l]))
        out_a, q, k, v = _mix_in(h, mix_pre_g[l:l + 1], w16(w_in[l]), sgu_norm_g[l], sgu_norm_b[l],
                                 sgu_w_s[l], sgu_b_s[l][:, :, None], sgu_out_g[l:l + 1])
        out_b = _sb_attention(q, k, v, batch, seq)
        kv = _mem_kv(mem2, mem_norm_g[l:l + 1], w16(xa_w_kv[l]))
        h = _post_mix(h, out_a, out_b, sb_out_g[l:l + 1], w16(w_out[l]), mix_post_g[l:l + 1],
                      xa_pre_g[l:l + 1], w16(xa_w_q[l]), kv, w16(xa_w_o[l]), xa_post_g[l:l + 1], seq)
        h = _ffn(h, ffn2_pre_g[l:l + 1], ffn2_post_g[l:l + 1],
                 w16(ffn2_w_gate[l]), w16(ffn2_w_up[l]), w16(ffn2_w_down[l]),
                 final_g=final_norm_g[l:l + 1])
    return h.reshape(batch, seq, d)
```

```python
import functools

import jax
import jax.numpy as jnp
from jax import lax
from jax.experimental import pallas as pl
from jax.experimental.pallas import tpu as pltpu

D_MODEL = 1024
MEM_LEN = 256
SGU_WIDTH = 512
SGU_GROUPS = 4
SGU_GROUP_DIM = SGU_WIDTH // SGU_GROUPS
CHUNK = 128
SB_WIDTH = 512
SB_HEAD_DIM = 64
XA_HEADS = 4
XA_HEAD_DIM = D_MODEL // XA_HEADS
D_FF = 2816
IN_COLS = 2 * SGU_WIDTH + 3 * SB_WIDTH
EPS = 1e-6

LANES = 128
VMEM_LIMIT = 56 * 1024 * 1024

FFN_ROWS = 1024
FFN_PARTS = 4
MIX_ROWS = 1024
MIX_PARTS = 4
POST_ROWS = 1024
POST_PARTS = 4
SB_TILE = 128
SB_SUBS = 8
SB_WINDOW = 3
SB_DONE = 105.0
SB_CLAMP = 80.0

F32 = jnp.float32
BF16 = jnp.bfloat16


def _rms(x, g):
    return x * lax.rsqrt(jnp.mean(x * x, axis=-1, keepdims=True) + EPS) * g


def _dot(a, b):
    return jnp.dot(a, b, preferred_element_type=F32)


def _dot_nt(a, b):
    return lax.dot_general(a, b, (((1,), (1,)), ((), ())), preferred_element_type=F32)


def _rows(tm, width):
    return pl.BlockSpec((tm, width), lambda i: (i, 0))


def _resident(shape):
    return pl.BlockSpec(shape, lambda *_: (0,) * len(shape), pipeline_mode=pl.Buffered(1))


def _params(*semantics):
    return pltpu.CompilerParams(dimension_semantics=semantics, vmem_limit_bytes=VMEM_LIMIT)


def _ffn_kernel(h_ref, pre_ref, post_ref, wg_ref, wu_ref, wd_ref, *rest, final_norm):
    o_ref = rest[-1]
    tm = h_ref.shape[0] // FFN_PARTS
    parts = [slice(p * tm, (p + 1) * tm) for p in range(FFN_PARTS)]

    def expand(rs):
        x = _rms(h_ref[rs, :], pre_ref[...]).astype(BF16)
        return _dot(x, wg_ref[...]), _dot(x, wu_ref[...])

    def contract(gate, up):
        return _dot((gate * jax.nn.sigmoid(gate) * up).astype(BF16), wd_ref[...])

    def finish(rs, f):
        y = h_ref[rs, :] + 0.5 * _rms(f, post_ref[...])
        if final_norm:
            y = _rms(y, rest[0][...])
        o_ref[rs, :] = y

    hidden = expand(parts[0])
    for p in range(FFN_PARTS):
        nxt = expand(parts[p + 1]) if p + 1 < FFN_PARTS else None
        f = contract(*hidden)
        if p > 0:
            finish(parts[p - 1], prev)
        prev, hidden = f, nxt
    finish(parts[-1], prev)


def _ffn(h, pre_g, post_g, w_gate, w_up, w_down, final_g=None):
    t, d = h.shape
    f = w_gate.shape[1]
    vec = _resident((1, d))
    in_specs = [_rows(FFN_ROWS, d), vec, vec, _resident((d, f)), _resident((d, f)), _resident((f, d))]
    args = [h, pre_g, post_g, w_gate, w_up, w_down]
    if final_g is not None:
        in_specs.append(vec)
        args.append(final_g)
    return pl.pallas_call(
        functools.partial(_ffn_kernel, final_norm=final_g is not None),
        grid=(t // FFN_ROWS,),
        in_specs=in_specs,
        out_specs=_rows(FFN_ROWS, d),
        out_shape=jax.ShapeDtypeStruct((t, d), F32),
        compiler_params=_params("parallel"),
        name="ffn_final" if final_g is not None else "ffn",
    )(*args)


def _mix_in_kernel(h_ref, g_ref, win_ref, ng_ref, nb_ref, ws_ref, bs_ref, og_ref,
                   oa_ref, q_ref, k_ref, v_ref):
    tm = h_ref.shape[0] // MIX_PARTS
    row = lax.broadcasted_iota(jnp.int32, (CHUNK, CHUNK), 0)
    col = lax.broadcasted_iota(jnp.int32, (CHUNK, CHUNK), 1)
    ws = [jnp.where(col <= row, ws_ref[g], 0.0).astype(BF16) for g in range(SGU_GROUPS)]

    def project(rs):
        return _dot(_rms(h_ref[rs, :], g_ref[...]).astype(BF16), win_ref[...])

    def gate(rs, proj):
        groups = []
        ss = jnp.zeros((tm, 1), F32)
        for g in range(SGU_GROUPS):
            lo = g * SGU_GROUP_DIM
            u = jax.nn.gelu(proj[:, lo:lo + SGU_GROUP_DIM])
            vg = jax.nn.gelu(proj[:, SGU_WIDTH + lo:SGU_WIDTH + lo + SGU_GROUP_DIM])
            mu = jnp.mean(vg, axis=-1, keepdims=True)
            dv = vg - mu
            var = jnp.mean(dv * dv, axis=-1, keepdims=True)
            vn = (dv * lax.rsqrt(var + EPS) * ng_ref[g:g + 1, :] + nb_ref[g:g + 1, :]).astype(BF16)
            mixed = jnp.concatenate(
                [_dot(ws[g], vn[c * CHUNK:(c + 1) * CHUNK, :]) + bs_ref[g] for c in range(tm // CHUNK)],
                axis=0)
            oa = u * mixed
            ss = ss + jnp.sum(oa * oa, axis=-1, keepdims=True)
            groups.append(oa)
        inv = lax.rsqrt(ss * (1.0 / SGU_WIDTH) + EPS)
        for g in range(SGU_GROUPS):
            lo = g * SGU_GROUP_DIM
            oa_ref[rs, lo:lo + SGU_GROUP_DIM] = (
                groups[g] * inv * og_ref[:, lo:lo + SGU_GROUP_DIM]).astype(BF16)
        base = 2 * SGU_WIDTH
        q_ref[rs, :] = (proj[:, base:base + SB_WIDTH] * (SB_HEAD_DIM ** -0.5)).astype(BF16)
        k_ref[rs, :] = proj[:, base + SB_WIDTH:base + 2 * SB_WIDTH].astype(BF16)
        v_ref[rs, :] = proj[:, base + 2 * SB_WIDTH:base + 3 * SB_WIDTH].astype(BF16)

    parts = [slice(p * tm, (p + 1) * tm) for p in range(MIX_PARTS)]
    proj = project(parts[0])
    for p in range(1, MIX_PARTS):
        nxt = project(parts[p])
        gate(parts[p - 1], proj)
        proj = nxt
    gate(parts[-1], proj)


def _mix_in(h, pre_g, w_in, norm_g, norm_b, w_s, b_s, out_g):
    t, d = h.shape
    half = jax.ShapeDtypeStruct((t, SB_WIDTH), BF16)
    return pl.pallas_call(
        _mix_in_kernel,
        grid=(t // MIX_ROWS,),
        in_specs=[_rows(MIX_ROWS, d), _resident((1, d)), _resident((d, IN_COLS)),
                  _resident((SGU_GROUPS, SGU_GROUP_DIM)), _resident((SGU_GROUPS, SGU_GROUP_DIM)),
                  _resident((SGU_GROUPS, CHUNK, CHUNK)), _resident((SGU_GROUPS, CHUNK, 1)),
                  _resident((1, SGU_WIDTH))],
        out_specs=[_rows(MIX_ROWS, SGU_WIDTH)] + [_rows(MIX_ROWS, SB_WIDTH)] * 3,
        out_shape=[jax.ShapeDtypeStruct((t, SGU_WIDTH), BF16), half, half, half],
        compiler_params=_params("parallel"),
        name="mix_in",
    )(h, pre_g, w_in, norm_g, norm_b, w_s, b_s, out_g)


def _sb_kernel(q_ref, k_ref, v_ref, o_ref, carry_ref, acc_ref):
    i = pl.program_id(2)
    lane = lax.broadcasted_iota(jnp.int32, (1, LANES), 1)
    first = lane < SB_HEAD_DIM

    r = lax.broadcasted_iota(jnp.int32, (SB_TILE, SB_TILE), 0)
    c = lax.broadcasted_iota(jnp.int32, (SB_TILE, SB_TILE), 1)
    suffix = jnp.concatenate([(r >= c).astype(BF16), jnp.ones((SB_TILE, SB_TILE), BF16)], axis=1)
    suffix = jnp.concatenate([suffix, suffix], axis=0)
    strict = c < r

    def q_heads(sub):
        q = q_ref[sub * SB_TILE:(sub + 1) * SB_TILE, :]
        zero = jnp.zeros_like(q)
        return jnp.where(first, q, zero), jnp.where(first, zero, q)

    def scores(qs, start, n, diagonal):
        kw = k_ref[pl.ds(start, n * SB_TILE), :]
        zs, split = [], []
        for qh in qs:
            z = _dot_nt(qh, kw)
            for t in range(n):
                zt = z[:, t * SB_TILE:(t + 1) * SB_TILE]
                sp = jnp.maximum(zt, jnp.log(1.0 + jnp.exp(jnp.minimum(zt, SB_CLAMP))))
                if diagonal and t == n - 1:
                    sp = jnp.where(strict, sp, 0.0)
                hi = sp.astype(BF16)
                lo = (sp - hi.astype(F32)).astype(BF16)
                zs.append(zt)
                split.append(jnp.concatenate([hi, lo], axis=1))
        return zs, jnp.concatenate(split, axis=0)

    def fold(zs, sums, start, n, carry, acc, diagonal):
        vw = v_ref[pl.ds(start, n * SB_TILE), :]
        weights, new_carry = [], []
        for hd, ch in enumerate(carry):
            parts = [None] * n
            for t in reversed(range(n)):
                blk = sums[(hd * n + t) * SB_TILE:(hd * n + t + 1) * SB_TILE]
                a = jnp.exp(zs[hd * n + t] - (ch + blk[:, :SB_TILE]))
                if diagonal and t == n - 1:
                    a = jnp.where(strict, a, 0.0)
                parts[t] = a.astype(BF16)
                ch = ch + blk[:, SB_TILE:]
            weights.append(jnp.concatenate(parts, axis=1))
            new_carry.append(ch)
        vzero = jnp.zeros_like(vw)
        v_heads = jnp.concatenate([jnp.where(first, vw, vzero), jnp.where(first, vzero, vw)], axis=0)
        return new_carry, acc + _dot(jnp.concatenate(weights, axis=1), v_heads)

    def save(sub, carry, acc):
        carry_ref[sub, 0] = carry[0]
        carry_ref[sub, 1] = carry[1]
        acc_ref[sub] = acc

    def first_windows(windows):
        zeros = jnp.zeros((SB_TILE, LANES), F32)
        staged = [scores(q_heads(sub), start, n, True) for sub, n, start in windows]
        sums = [_dot(split, suffix) for _, split in staged]
        for (sub, n, start), (zs, _), sm in zip(windows, staged, sums):
            save(sub, *fold(zs, sm, start, n, (zeros, zeros), zeros, True))

    @pl.when(i == 0)
    def _():
        first_windows([(sub, min(sub + 1, SB_WINDOW), max(sub + 1 - SB_WINDOW, 0) * SB_TILE)
                       for sub in range(SB_SUBS)])

    @pl.when(i > 0)
    def _():
        first_windows([(sub, SB_WINDOW,
                        pl.multiple_of((i * SB_SUBS + sub - (SB_WINDOW - 1)) * SB_TILE, SB_TILE))
                       for sub in range(SB_SUBS)])

    lowest = carry_ref[0, 0]
    for sub in range(SB_SUBS):
        for hd in range(2):
            lowest = jnp.minimum(lowest, carry_ref[sub, hd])

    @pl.when(jnp.min(lowest) < SB_DONE)
    def _():
        for sub in range(SB_SUBS):
            diag_tile = i * SB_SUBS + sub
            qs = q_heads(sub)

            def cond(state):
                j, lowest_carry = state
                return jnp.logical_and(j >= 0, lowest_carry < SB_DONE)

            def body(state, sub=sub, qs=qs):
                j, _ = state
                start = pl.multiple_of(j * SB_TILE, SB_TILE)
                zs, split = scores(qs, start, 1, False)
                carry, acc = fold(zs, _dot(split, suffix), start, 1,
                                  (carry_ref[sub, 0], carry_ref[sub, 1]), acc_ref[sub], False)
                save(sub, carry, acc)
                return j - 1, jnp.min(jnp.minimum(carry[0], carry[1]))

            done_tiles = jnp.minimum(diag_tile + 1, SB_WINDOW)
            lax.while_loop(cond, body, (diag_tile - done_tiles,
                                        jnp.min(jnp.minimum(carry_ref[sub, 0], carry_ref[sub, 1]))))

    for sub in range(SB_SUBS):
        o_ref[sub * SB_TILE:(sub + 1) * SB_TILE, :] = acc_ref[sub]


def _sb_attention(q, k, v, batch, seq):
    t, w = q.shape
    rows = SB_SUBS * SB_TILE
    nq = seq // rows
    kv_spec = pl.BlockSpec((seq, LANES), lambda b, p, i: (b, p))
    return pl.pallas_call(
        _sb_kernel,
        grid=(batch, w // LANES, nq),
        in_specs=[pl.BlockSpec((rows, LANES), lambda b, p, i: (b * nq + i, p)), kv_spec, kv_spec],
        out_specs=pl.BlockSpec((rows, LANES), lambda b, p, i: (b * nq + i, p)),
        out_shape=jax.ShapeDtypeStruct((t, w), F32),
        scratch_shapes=[pltpu.VMEM((SB_SUBS, 2, SB_TILE, LANES), F32),
                        pltpu.VMEM((SB_SUBS, SB_TILE, LANES), F32)],
        compiler_params=_params("parallel", "parallel", "arbitrary"),
        name="sb_attn",
    )(q, k, v)


def _mem_kv_kernel(mem_ref, g_ref, w_ref, o_ref):
    o_ref[...] = _dot(_rms(mem_ref[...], g_ref[...]).astype(BF16), w_ref[...]).astype(BF16)


def _mem_kv(mem, g, w_kv):
    t, d = mem.shape
    n = w_kv.shape[1]
    return pl.pallas_call(
        _mem_kv_kernel,
        grid=(t // MEM_LEN,),
        in_specs=[_rows(MEM_LEN, d), _resident((1, d)), _resident((d, n))],
        out_specs=_rows(MEM_LEN, n),
        out_shape=jax.ShapeDtypeStruct((t, n), BF16),
        compiler_params=_params("parallel"),
        name="mem_kv",
    )(mem, g, w_kv)


def _post_mix_kernel(h_ref, oa_ref, ob_ref, sbg_ref, wout_ref, mpost_ref, xpre_ref, wq_ref,
                     kv_ref, wo_ref, xpost_ref, o_ref):
    tm = h_ref.shape[0] // POST_PARTS
    parts = [slice(p * tm, (p + 1) * tm) for p in range(POST_PARTS)]
    heads = [slice(hd * XA_HEAD_DIM, (hd + 1) * XA_HEAD_DIM) for hd in range(XA_HEADS)]

    merged = []
    for rs in parts:
        ob = _rms(ob_ref[rs, :], sbg_ref[...]).astype(BF16)
        merged.append(_dot(oa_ref[rs, :], wout_ref[:SGU_WIDTH, :]) + _dot(ob, wout_ref[SGU_WIDTH:, :]))
    hs, qs = [], []
    for rs, m in zip(parts, merged):
        h = h_ref[rs, :] + _rms(m, mpost_ref[...])
        x = _rms(h, xpre_ref[...]).astype(BF16)
        hs.append(h)
        qs.append((_dot(x, wq_ref[...]) * (XA_HEAD_DIM ** -0.5)).astype(BF16))
    logits = [[_dot_nt(q[:, hd], kv_ref[:, hd]) for hd in heads] for q in qs]
    attended = []
    for part in logits:
        outs = []
        for hd, lg in zip(heads, part):
            e = jnp.exp(lg - jnp.max(lg, axis=-1, keepdims=True))
            o = _dot(e.astype(BF16), kv_ref[:, D_MODEL + hd.start:D_MODEL + hd.stop])
            outs.append((o / jnp.sum(e, axis=-1, keepdims=True)).astype(BF16))
        attended.append(jnp.concatenate(outs, axis=1))
    cs = [_dot(a, wo_ref[...]) for a in attended]
    for rs, h, c in zip(parts, hs, cs):
        o_ref[rs, :] = h + _rms(c, xpost_ref[...])


def _post_mix(h, out_a, out_b, sb_g, w_out, mix_post_g, xa_pre_g, w_q, kv, w_o, xa_post_g, seq):
    t, d = h.shape
    per_batch = seq // POST_ROWS
    vec = _resident((1, d))
    return pl.pallas_call(
        _post_mix_kernel,
        grid=(t // POST_ROWS,),
        in_specs=[_rows(POST_ROWS, d), _rows(POST_ROWS, SGU_WIDTH), _rows(POST_ROWS, SB_WIDTH),
                  _resident((1, SB_WIDTH)), _resident((d, d)), vec, vec, _resident((d, d)),
                  pl.BlockSpec((MEM_LEN, 2 * d), lambda i: (i // per_batch, 0)),
                  _resident((d, d)), vec],
        out_specs=_rows(POST_ROWS, d),
        out_shape=jax.ShapeDtypeStruct((t, d), F32),
        compiler_params=_params("parallel"),
        name="post_mix",
    )(h, out_a, out_b, sb_g, w_out, mix_post_g, xa_pre_g, w_q, kv, w_o, xa_post_g)


def kernel(x, mem, ffn1_pre_g, ffn1_post_g, ffn1_w_gate, ffn1_w_up, ffn1_w_down, mix_pre_g, mix_post_g, w_in, sgu_norm_g, sgu_norm_b, sgu_w_s, sgu_b_s, sgu_out_g, sb_out_g, w_out, xa_pre_g, xa_post_g, mem_norm_g, xa_w_q, xa_w_kv, xa_w_o, ffn2_pre_g, ffn2_post_g, ffn2_w_gate, ffn2_w_up, ffn2_w_down, final_norm_g):
    batch, seq, d = x.shape
    depth = ffn1_pre_g.shape[0]
    h = x.reshape(batch * seq, d)
    mem2 = mem.reshape(batch * mem.shape[1], d)

    def w16(w):
        return w.astype(BF16)

    for l in range(depth):
        h = _ffn(h, ffn1_pre_g[l:l + 1], ffn1_post_g[l:l + 1],
                 w16(ffn1_w_gate[l]), w16(ffn1_w_up[l]), w16(ffn1_w_down[l]))
        out_a, q, k, v = _mix_in(h, mix_pre_g[l:l + 1], w16(w_in[l]), sgu_norm_g[l], sgu_norm_b[l],
                                 sgu_w_s[l], sgu_b_s[l][:, :, None], sgu_out_g[l:l + 1])
        out_b = _sb_attention(q, k, v, batch, seq)
        kv = _mem_kv(mem2, mem_norm_g[l:l + 1], w16(xa_w_kv[l]))
        h = _post_mix(h, out_a, out_b, sb_out_g[l:l + 1], w16(w_out[l]), mix_post_g[l:l + 1],
                      xa_pre_g[l:l + 1], w16(xa_w_q[l]), kv, w16(xa_w_o[l]), xa_post_g[l:l + 1], seq)
        h = _ffn(h, ffn2_pre_g[l:l + 1], ffn2_post_g[l:l + 1],
                 w16(ffn2_w_gate[l]), w16(ffn2_w_up[l]), w16(ffn2_w_down[l]),
                 final_g=final_norm_g[l:l + 1])
    return h.reshape(batch, seq, d)
```

```python
import functools

import jax
import jax.numpy as jnp
from jax import lax
from jax.experimental import pallas as pl
from jax.experimental.pallas import tpu as pltpu

D_MODEL = 1024
MEM_LEN = 256
SGU_WIDTH = 512
SGU_GROUPS = 4
SGU_GROUP_DIM = SGU_WIDTH // SGU_GROUPS
CHUNK = 128
SB_WIDTH = 512
SB_HEAD_DIM = 64
XA_HEADS = 4
XA_HEAD_DIM = D_MODEL // XA_HEADS
D_FF = 2816
IN_COLS = 2 * SGU_WIDTH + 3 * SB_WIDTH
EPS = 1e-6

LANES = 128
VMEM_LIMIT = 56 * 1024 * 1024

FFN_ROWS = 1024
FFN_PARTS = 4
MIX_ROWS = 1024
MIX_PARTS = 4
POST_ROWS = 1024
POST_PARTS = 4
SB_TILE = 128
SB_ROWS = 64
SB_SUBS = 16
SB_WINDOW = 2
SB_DONE = 105.0
SB_CLAMP = 80.0

F32 = jnp.float32
BF16 = jnp.bfloat16


def _rms(x, g):
    return x * lax.rsqrt(jnp.mean(x * x, axis=-1, keepdims=True) + EPS) * g


def _dot(a, b):
    return jnp.dot(a, b, preferred_element_type=F32)


def _dot_nt(a, b):
    return lax.dot_general(a, b, (((1,), (1,)), ((), ())), preferred_element_type=F32)


def _rows(tm, width):
    return pl.BlockSpec((tm, width), lambda i: (i, 0))


def _resident(shape):
    return pl.BlockSpec(shape, lambda *_: (0,) * len(shape), pipeline_mode=pl.Buffered(1))


def _params(*semantics):
    return pltpu.CompilerParams(dimension_semantics=semantics, vmem_limit_bytes=VMEM_LIMIT)


def _ffn_kernel(h_ref, pre_ref, post_ref, wg_ref, wu_ref, wd_ref, *rest, final_norm):
    o_ref = rest[-1]
    tm = h_ref.shape[0] // FFN_PARTS
    parts = [slice(p * tm, (p + 1) * tm) for p in range(FFN_PARTS)]

    def expand(rs):
        x = _rms(h_ref[rs, :], pre_ref[...]).astype(BF16)
        return _dot(x, wg_ref[...]), _dot(x, wu_ref[...])

    def contract(gate, up):
        return _dot((gate * jax.nn.sigmoid(gate) * up).astype(BF16), wd_ref[...])

    def finish(rs, f):
        y = h_ref[rs, :] + 0.5 * _rms(f, post_ref[...])
        if final_norm:
            y = _rms(y, rest[0][...])
        o_ref[rs, :] = y

    hidden = expand(parts[0])
    for p in range(FFN_PARTS):
        nxt = expand(parts[p + 1]) if p + 1 < FFN_PARTS else None
        f = contract(*hidden)
        if p > 0:
            finish(parts[p - 1], prev)
        prev, hidden = f, nxt
    finish(parts[-1], prev)


def _ffn(h, pre_g, post_g, w_gate, w_up, w_down, final_g=None):
    t, d = h.shape
    f = w_gate.shape[1]
    vec = _resident((1, d))
    in_specs = [_rows(FFN_ROWS, d), vec, vec, _resident((d, f)), _resident((d, f)), _resident((f, d))]
    args = [h, pre_g, post_g, w_gate, w_up, w_down]
    if final_g is not None:
        in_specs.append(vec)
        args.append(final_g)
    return pl.pallas_call(
        functools.partial(_ffn_kernel, final_norm=final_g is not None),
        grid=(t // FFN_ROWS,),
        in_specs=in_specs,
        out_specs=_rows(FFN_ROWS, d),
        out_shape=jax.ShapeDtypeStruct((t, d), F32),
        compiler_params=_params("parallel"),
        name="ffn_final" if final_g is not None else "ffn",
    )(*args)


def _mix_in_kernel(h_ref, g_ref, win_ref, ng_ref, nb_ref, ws_ref, bs_ref, og_ref,
                   oa_ref, q_ref, k_ref, v_ref):
    tm = h_ref.shape[0] // MIX_PARTS
    row = lax.broadcasted_iota(jnp.int32, (CHUNK, CHUNK), 0)
    col = lax.broadcasted_iota(jnp.int32, (CHUNK, CHUNK), 1)
    ws = [jnp.where(col <= row, ws_ref[g], 0.0).astype(BF16) for g in range(SGU_GROUPS)]

    def project(rs):
        return _dot(_rms(h_ref[rs, :], g_ref[...]).astype(BF16), win_ref[...])

    def gate(rs, proj):
        groups = []
        ss = jnp.zeros((tm, 1), F32)
        for g in range(SGU_GROUPS):
            lo = g * SGU_GROUP_DIM
            u = jax.nn.gelu(proj[:, lo:lo + SGU_GROUP_DIM])
            vg = jax.nn.gelu(proj[:, SGU_WIDTH + lo:SGU_WIDTH + lo + SGU_GROUP_DIM])
            mu = jnp.mean(vg, axis=-1, keepdims=True)
            dv = vg - mu
            var = jnp.mean(dv * dv, axis=-1, keepdims=True)
            vn = (dv * lax.rsqrt(var + EPS) * ng_ref[g:g + 1, :] + nb_ref[g:g + 1, :]).astype(BF16)
            mixed = jnp.concatenate(
                [_dot(ws[g], vn[c * CHUNK:(c + 1) * CHUNK, :]) + bs_ref[g] for c in range(tm // CHUNK)],
                axis=0)
            oa = u * mixed
            ss = ss + jnp.sum(oa * oa, axis=-1, keepdims=True)
            groups.append(oa)
        inv = lax.rsqrt(ss * (1.0 / SGU_WIDTH) + EPS)
        for g in range(SGU_GROUPS):
            lo = g * SGU_GROUP_DIM
            oa_ref[rs, lo:lo + SGU_GROUP_DIM] = (
                groups[g] * inv * og_ref[:, lo:lo + SGU_GROUP_DIM]).astype(BF16)
        base = 2 * SGU_WIDTH
        q_ref[rs, :] = (proj[:, base:base + SB_WIDTH] * (SB_HEAD_DIM ** -0.5)).astype(BF16)
        k_ref[rs, :] = proj[:, base + SB_WIDTH:base + 2 * SB_WIDTH].astype(BF16)
        v_ref[rs, :] = proj[:, base + 2 * SB_WIDTH:base + 3 * SB_WIDTH].astype(BF16)

    parts = [slice(p * tm, (p + 1) * tm) for p in range(MIX_PARTS)]
    proj = project(parts[0])
    for p in range(1, MIX_PARTS):
        nxt = project(parts[p])
        gate(parts[p - 1], proj)
        proj = nxt
    gate(parts[-1], proj)


def _mix_in(h, pre_g, w_in, norm_g, norm_b, w_s, b_s, out_g):
    t, d = h.shape
    half = jax.ShapeDtypeStruct((t, SB_WIDTH), BF16)
    return pl.pallas_call(
        _mix_in_kernel,
        grid=(t // MIX_ROWS,),
        in_specs=[_rows(MIX_ROWS, d), _resident((1, d)), _resident((d, IN_COLS)),
                  _resident((SGU_GROUPS, SGU_GROUP_DIM)), _resident((SGU_GROUPS, SGU_GROUP_DIM)),
                  _resident((SGU_GROUPS, CHUNK, CHUNK)), _resident((SGU_GROUPS, CHUNK, 1)),
                  _resident((1, SGU_WIDTH))],
        out_specs=[_rows(MIX_ROWS, SGU_WIDTH)] + [_rows(MIX_ROWS, SB_WIDTH)] * 3,
        out_shape=[jax.ShapeDtypeStruct((t, SGU_WIDTH), BF16), half, half, half],
        compiler_params=_params("parallel"),
        name="mix_in",
    )(h, pre_g, w_in, norm_g, norm_b, w_s, b_s, out_g)


def _sb_kernel(q_ref, k_ref, v_ref, o_ref, carry_ref):
    i = pl.program_id(2)
    lane = lax.broadcasted_iota(jnp.int32, (1, LANES), 1)
    first = lane < SB_HEAD_DIM

    r = lax.broadcasted_iota(jnp.int32, (SB_TILE, SB_TILE), 0)
    c = lax.broadcasted_iota(jnp.int32, (SB_TILE, SB_TILE), 1)
    suffix = jnp.concatenate([(r >= c).astype(BF16), jnp.ones((SB_TILE, SB_TILE), BF16)], axis=1)
    suffix = jnp.concatenate([suffix, suffix], axis=0)
    qrow = lax.broadcasted_iota(jnp.int32, (SB_ROWS, SB_TILE), 0)
    kcol = lax.broadcasted_iota(jnp.int32, (SB_ROWS, SB_TILE), 1)

    def rows(sub):
        return slice(sub * SB_ROWS, (sub + 1) * SB_ROWS)

    def q_heads(sub):
        q = q_ref[rows(sub), :]
        zero = jnp.zeros_like(q)
        return jnp.where(first, q, zero), jnp.where(first, zero, q)

    def scores(qs, start, n, valid):
        kw = k_ref[pl.ds(start, n * SB_TILE), :]
        zs, split = [], []
        for qh in qs:
            z = _dot_nt(qh, kw)
            for t in range(n):
                zt = z[:, t * SB_TILE:(t + 1) * SB_TILE]
                sp = jnp.maximum(zt, jnp.log(1.0 + jnp.exp(jnp.minimum(zt, SB_CLAMP))))
                if t == n - 1:
                    sp = jnp.where(valid, sp, 0.0)
                hi = sp.astype(BF16)
                lo = (sp - hi.astype(F32)).astype(BF16)
                zs.append(zt)
                split.append(jnp.concatenate([hi, lo], axis=1))
        return zs, jnp.concatenate(split, axis=0)

    def fold(zs, sums, start, n, valid, carry, acc):
        vw = v_ref[pl.ds(start, n * SB_TILE), :]
        weights, new_carry = [], []
        for hd, ch in enumerate(carry):
            parts = [None] * n
            for t in reversed(range(n)):
                blk = sums[(hd * n + t) * SB_ROWS:(hd * n + t + 1) * SB_ROWS]
                a = jnp.exp(zs[hd * n + t] - (ch + blk[:, :SB_TILE]))
                if t == n - 1:
                    a = jnp.where(valid, a, 0.0)
                parts[t] = a.astype(BF16)
                ch = ch + blk[:, SB_TILE:]
            weights.append(jnp.concatenate(parts, axis=1))
            new_carry.append(ch)
        vzero = jnp.zeros_like(vw)
        v_heads = jnp.concatenate([jnp.where(first, vw, vzero), jnp.where(first, vzero, vw)], axis=0)
        return new_carry, acc + _dot(jnp.concatenate(weights, axis=1), v_heads)

    def save(sub, carry, acc):
        carry_ref[sub, 0] = carry[0]
        carry_ref[sub, 1] = carry[1]
        o_ref[rows(sub), :] = acc

    def first_windows(windows):
        zeros = jnp.zeros((SB_ROWS, LANES), F32)
        masks = {lead: kcol < qrow + lead for lead in sorted({w[3] for w in windows})}
        staged = [scores(q_heads(sub), start, n, masks[lead]) for sub, n, start, lead in windows]
        sums = [_dot(split, suffix) for _, split in staged]
        for (sub, n, start, lead), (zs, _), sm in zip(windows, staged, sums):
            save(sub, *fold(zs, sm, start, n, masks[lead], (zeros, zeros), zeros))

    def static_window(sub):
        row0 = sub * SB_ROWS
        n = min(SB_WINDOW, -(-(row0 + SB_ROWS) // SB_TILE))
        start = max(row0 + SB_ROWS - SB_WINDOW * SB_TILE, 0)
        return sub, n, start, row0 - start - (n - 1) * SB_TILE

    @pl.when(i == 0)
    def _():
        first_windows([static_window(sub) for sub in range(SB_SUBS)])

    @pl.when(i > 0)
    def _():
        first_windows([(sub, SB_WINDOW,
                        pl.multiple_of((i * SB_SUBS + sub + 1) * SB_ROWS - SB_WINDOW * SB_TILE, SB_ROWS),
                        SB_TILE - SB_ROWS)
                       for sub in range(SB_SUBS)])

    lowest = carry_ref[0, 0]
    for sub in range(SB_SUBS):
        for hd in range(2):
            lowest = jnp.minimum(lowest, carry_ref[sub, hd])

    @pl.when(jnp.min(lowest) < SB_DONE)
    def _():
        for sub in range(SB_SUBS):
            qs = q_heads(sub)

            def cond(state):
                end, lowest_carry = state
                return jnp.logical_and(end > 0, lowest_carry < SB_DONE)

            def body(state, sub=sub, qs=qs):
                end, _ = state
                start = pl.multiple_of(jnp.maximum(end - SB_TILE, 0), SB_ROWS)
                valid = kcol < end - start
                zs, split = scores(qs, start, 1, valid)
                carry, acc = fold(zs, _dot(split, suffix), start, 1, valid,
                                  (carry_ref[sub, 0], carry_ref[sub, 1]), o_ref[rows(sub), :])
                save(sub, carry, acc)
                return start, jnp.min(jnp.minimum(carry[0], carry[1]))

            folded_from = jnp.maximum((i * SB_SUBS + sub + 1) * SB_ROWS - SB_WINDOW * SB_TILE, 0)
            lax.while_loop(cond, body, (folded_from,
                                        jnp.min(jnp.minimum(carry_ref[sub, 0], carry_ref[sub, 1]))))


def _sb_attention(q, k, v, batch, seq):
    t, w = q.shape
    block = SB_SUBS * SB_ROWS
    nq = seq // block
    kv_spec = pl.BlockSpec((seq, LANES), lambda b, p, i: (b, p))
    return pl.pallas_call(
        _sb_kernel,
        grid=(batch, w // LANES, nq),
        in_specs=[pl.BlockSpec((block, LANES), lambda b, p, i: (b * nq + i, p)), kv_spec, kv_spec],
        out_specs=pl.BlockSpec((block, LANES), lambda b, p, i: (b * nq + i, p)),
        out_shape=jax.ShapeDtypeStruct((t, w), F32),
        scratch_shapes=[pltpu.VMEM((SB_SUBS, 2, SB_ROWS, LANES), F32)],
        compiler_params=_params("parallel", "parallel", "arbitrary"),
        name="sb_attn",
    )(q, k, v)


def _mem_kv_kernel(mem_ref, g_ref, w_ref, o_ref):
    o_ref[...] = _dot(_rms(mem_ref[...], g_ref[...]).astype(BF16), w_ref[...]).astype(BF16)


def _mem_kv(mem, g, w_kv):
    t, d = mem.shape
    n = w_kv.shape[1]
    return pl.pallas_call(
        _mem_kv_kernel,
        grid=(t // MEM_LEN,),
        in_specs=[_rows(MEM_LEN, d), _resident((1, d)), _resident((d, n))],
        out_specs=_rows(MEM_LEN, n),
        out_shape=jax.ShapeDtypeStruct((t, n), BF16),
        compiler_params=_params("parallel"),
        name="mem_kv",
    )(mem, g, w_kv)


def _post_mix_kernel(h_ref, oa_ref, ob_ref, sbg_ref, wout_ref, mpost_ref, xpre_ref, wq_ref,
                     kv_ref, wo_ref, xpost_ref, o_ref):
    tm = h_ref.shape[0] // POST_PARTS
    parts = [slice(p * tm, (p + 1) * tm) for p in range(POST_PARTS)]
    heads = [slice(hd * XA_HEAD_DIM, (hd + 1) * XA_HEAD_DIM) for hd in range(XA_HEADS)]

    merged = []
    for rs in parts:
        ob = _rms(ob_ref[rs, :], sbg_ref[...]).astype(BF16)
        merged.append(_dot(oa_ref[rs, :], wout_ref[:SGU_WIDTH, :]) + _dot(ob, wout_ref[SGU_WIDTH:, :]))
    hs, qs = [], []
    for rs, m in zip(parts, merged):
        h = h_ref[rs, :] + _rms(m, mpost_ref[...])
        x = _rms(h, xpre_ref[...]).astype(BF16)
        hs.append(h)
        qs.append((_dot(x, wq_ref[...]) * (XA_HEAD_DIM ** -0.5)).astype(BF16))
    logits = [[_dot_nt(q[:, hd], kv_ref[:, hd]) for hd in heads] for q in qs]
    attended = []
    for part in logits:
        outs = []
        for hd, lg in zip(heads, part):
            e = jnp.exp(lg - jnp.max(lg, axis=-1, keepdims=True))
            o = _dot(e.astype(BF16), kv_ref[:, D_MODEL + hd.start:D_MODEL + hd.stop])
            outs.append((o / jnp.sum(e, axis=-1, keepdims=True)).astype(BF16))
        attended.append(jnp.concatenate(outs, axis=1))
    cs = [_dot(a, wo_ref[...]) for a in attended]
    for rs, h, c in zip(parts, hs, cs):
        o_ref[rs, :] = h + _rms(c, xpost_ref[...])


def _post_mix(h, out_a, out_b, sb_g, w_out, mix_post_g, xa_pre_g, w_q, kv, w_o, xa_post_g, seq):
    t, d = h.shape
    per_batch = seq // POST_ROWS
    vec = _resident((1, d))
    return pl.pallas_call(
        _post_mix_kernel,
        grid=(t // POST_ROWS,),
        in_specs=[_rows(POST_ROWS, d), _rows(POST_ROWS, SGU_WIDTH), _rows(POST_ROWS, SB_WIDTH),
                  _resident((1, SB_WIDTH)), _resident((d, d)), vec, vec, _resident((d, d)),
                  pl.BlockSpec((MEM_LEN, 2 * d), lambda i: (i // per_batch, 0)),
                  _resident((d, d)), vec],
        out_specs=_rows(POST_ROWS, d),
        out_shape=jax.ShapeDtypeStruct((t, d), F32),
        compiler_params=_params("parallel"),
        name="post_mix",
    )(h, out_a, out_b, sb_g, w_out, mix_post_g, xa_pre_g, w_q, kv, w_o, xa_post_g)


def kernel(x, mem, ffn1_pre_g, ffn1_post_g, ffn1_w_gate, ffn1_w_up, ffn1_w_down, mix_pre_g, mix_post_g, w_in, sgu_norm_g, sgu_norm_b, sgu_w_s, sgu_b_s, sgu_out_g, sb_out_g, w_out, xa_pre_g, xa_post_g, mem_norm_g, xa_w_q, xa_w_kv, xa_w_o, ffn2_pre_g, ffn2_post_g, ffn2_w_gate, ffn2_w_up, ffn2_w_down, final_norm_g):
    batch, seq, d = x.shape
    depth = ffn1_pre_g.shape[0]
    h = x.reshape(batch * seq, d)
    mem2 = mem.reshape(batch * mem.shape[1], d)

    def w16(w):
        return w.astype(BF16)

    for l in range(depth):
        h = _ffn(h, ffn1_pre_g[l:l + 1], ffn1_post_g[l:l + 1],
                 w16(ffn1_w_gate[l]), w16(ffn1_w_up[l]), w16(ffn1_w_down[l]))
        out_a, q, k, v = _mix_in(h, mix_pre_g[l:l + 1], w16(w_in[l]), sgu_norm_g[l], sgu_norm_b[l],
                                 sgu_w_s[l], sgu_b_s[l][:, :, None], sgu_out_g[l:l + 1])
        out_b = _sb_attention(q, k, v, batch, seq)
        kv = _mem_kv(mem2, mem_norm_g[l:l + 1], w16(xa_w_kv[l]))
        h = _post_mix(h, out_a, out_b, sb_out_g[l:l + 1], w16(w_out[l]), mix_post_g[l:l + 1],
                      xa_pre_g[l:l + 1], w16(xa_w_q[l]), kv, w16(xa_w_o[l]), xa_post_g[l:l + 1], seq)
        h = _ffn(h, ffn2_pre_g[l:l + 1], ffn2_post_g[l:l + 1],
                 w16(ffn2_w_gate[l]), w16(ffn2_w_up[l]), w16(ffn2_w_down[l]),
                 final_g=final_norm_g[l:l + 1])
    return h.reshape(batch, seq, d)
```

```python
import functools

import jax
import jax.numpy as jnp
from jax import lax
from jax.experimental import pallas as pl
from jax.experimental.pallas import tpu as pltpu

D_MODEL = 1024
MEM_LEN = 256
SGU_WIDTH = 512
SGU_GROUPS = 4
SGU_GROUP_DIM = SGU_WIDTH // SGU_GROUPS
CHUNK = 128
SB_WIDTH = 512
SB_HEAD_DIM = 64
XA_HEADS = 4
XA_HEAD_DIM = D_MODEL // XA_HEADS
D_FF = 2816
IN_COLS = 2 * SGU_WIDTH + 3 * SB_WIDTH
EPS = 1e-6

LANES = 128
SB_PAIRS = SB_WIDTH // LANES
VMEM_LIMIT = 56 * 1024 * 1024

FFN_ROWS = 1024
FFN_PARTS = 4
MIX_ROWS = 1024
MIX_PARTS = 4
POST_ROWS = 1024
POST_PARTS = 4
SB_TILE = 128
SB_ROWS = 64
SB_SUBS = 16
SB_WINDOW = 2
SB_DONE = 105.0
SB_CLAMP = 80.0

F32 = jnp.float32
BF16 = jnp.bfloat16


def _rms(x, g):
    return x * lax.rsqrt(jnp.mean(x * x, axis=-1, keepdims=True) + EPS) * g


def _dot(a, b):
    return jnp.dot(a, b, preferred_element_type=F32)


def _dot_nt(a, b):
    return lax.dot_general(a, b, (((1,), (1,)), ((), ())), preferred_element_type=F32)


def _rows(tm, width):
    return pl.BlockSpec((tm, width), lambda i: (i, 0))


def _resident(shape):
    return pl.BlockSpec(shape, lambda *_: (0,) * len(shape), pipeline_mode=pl.Buffered(1))


def _params(*semantics):
    return pltpu.CompilerParams(dimension_semantics=semantics, vmem_limit_bytes=VMEM_LIMIT)


def _ffn_kernel(h_ref, pre_ref, post_ref, wg_ref, wu_ref, wd_ref, *rest, final_norm):
    o_ref = rest[-1]
    tm = h_ref.shape[0] // FFN_PARTS
    parts = [slice(p * tm, (p + 1) * tm) for p in range(FFN_PARTS)]

    def expand(rs):
        x = _rms(h_ref[rs, :], pre_ref[...]).astype(BF16)
        return _dot(x, wg_ref[...]), _dot(x, wu_ref[...])

    def contract(gate, up):
        return _dot((gate * jax.nn.sigmoid(gate) * up).astype(BF16), wd_ref[...])

    def finish(rs, f):
        y = h_ref[rs, :] + 0.5 * _rms(f, post_ref[...])
        if final_norm:
            y = _rms(y, rest[0][...])
        o_ref[rs, :] = y

    hidden = expand(parts[0])
    for p in range(FFN_PARTS):
        nxt = expand(parts[p + 1]) if p + 1 < FFN_PARTS else None
        f = contract(*hidden)
        if p > 0:
            finish(parts[p - 1], prev)
        prev, hidden = f, nxt
    finish(parts[-1], prev)


def _ffn(h, pre_g, post_g, w_gate, w_up, w_down, final_g=None):
    t, d = h.shape
    f = w_gate.shape[1]
    vec = _resident((1, d))
    in_specs = [_rows(FFN_ROWS, d), vec, vec, _resident((d, f)), _resident((d, f)), _resident((f, d))]
    args = [h, pre_g, post_g, w_gate, w_up, w_down]
    if final_g is not None:
        in_specs.append(vec)
        args.append(final_g)
    return pl.pallas_call(
        functools.partial(_ffn_kernel, final_norm=final_g is not None),
        grid=(t // FFN_ROWS,),
        in_specs=in_specs,
        out_specs=_rows(FFN_ROWS, d),
        out_shape=jax.ShapeDtypeStruct((t, d), F32),
        compiler_params=_params("parallel"),
        name="ffn_final" if final_g is not None else "ffn",
    )(*args)


def _mix_in_kernel(h_ref, g_ref, win_ref, ng_ref, nb_ref, ws_ref, bs_ref, og_ref,
                   oa_ref, q_ref, k_ref, v_ref):
    tm = h_ref.shape[0] // MIX_PARTS
    row = lax.broadcasted_iota(jnp.int32, (CHUNK, CHUNK), 0)
    col = lax.broadcasted_iota(jnp.int32, (CHUNK, CHUNK), 1)
    ws = [jnp.where(col <= row, ws_ref[g], 0.0).astype(BF16) for g in range(SGU_GROUPS)]

    def project(rs):
        return _dot(_rms(h_ref[rs, :], g_ref[...]).astype(BF16), win_ref[...])

    def gate(rs, proj):
        groups = []
        ss = jnp.zeros((tm, 1), F32)
        for g in range(SGU_GROUPS):
            lo = g * SGU_GROUP_DIM
            u = jax.nn.gelu(proj[:, lo:lo + SGU_GROUP_DIM])
            vg = jax.nn.gelu(proj[:, SGU_WIDTH + lo:SGU_WIDTH + lo + SGU_GROUP_DIM])
            mu = jnp.mean(vg, axis=-1, keepdims=True)
            dv = vg - mu
            var = jnp.mean(dv * dv, axis=-1, keepdims=True)
            vn = (dv * lax.rsqrt(var + EPS) * ng_ref[g:g + 1, :] + nb_ref[g:g + 1, :]).astype(BF16)
            mixed = jnp.concatenate(
                [_dot(ws[g], vn[c * CHUNK:(c + 1) * CHUNK, :]) + bs_ref[g] for c in range(tm // CHUNK)],
                axis=0)
            oa = u * mixed
            ss = ss + jnp.sum(oa * oa, axis=-1, keepdims=True)
            groups.append(oa)
        inv = lax.rsqrt(ss * (1.0 / SGU_WIDTH) + EPS)
        for g in range(SGU_GROUPS):
            lo = g * SGU_GROUP_DIM
            oa_ref[rs, lo:lo + SGU_GROUP_DIM] = (
                groups[g] * inv * og_ref[:, lo:lo + SGU_GROUP_DIM]).astype(BF16)
        base = 2 * SGU_WIDTH
        for p in range(SB_PAIRS):
            lo = base + p * LANES
            q_ref[p, rs, :] = (proj[:, lo:lo + LANES] * (SB_HEAD_DIM ** -0.5)).astype(BF16)
            k_ref[p, rs, :] = proj[:, lo + SB_WIDTH:lo + SB_WIDTH + LANES].astype(BF16)
            v_ref[p, rs, :] = proj[:, lo + 2 * SB_WIDTH:lo + 2 * SB_WIDTH + LANES].astype(BF16)

    parts = [slice(p * tm, (p + 1) * tm) for p in range(MIX_PARTS)]
    proj = project(parts[0])
    for p in range(1, MIX_PARTS):
        nxt = project(parts[p])
        gate(parts[p - 1], proj)
        proj = nxt
    gate(parts[-1], proj)


def _mix_in(h, pre_g, w_in, norm_g, norm_b, w_s, b_s, out_g):
    t, d = h.shape
    half = jax.ShapeDtypeStruct((SB_PAIRS, t, LANES), BF16)
    half_spec = pl.BlockSpec((SB_PAIRS, MIX_ROWS, LANES), lambda i: (0, i, 0))
    return pl.pallas_call(
        _mix_in_kernel,
        grid=(t // MIX_ROWS,),
        in_specs=[_rows(MIX_ROWS, d), _resident((1, d)), _resident((d, IN_COLS)),
                  _resident((SGU_GROUPS, SGU_GROUP_DIM)), _resident((SGU_GROUPS, SGU_GROUP_DIM)),
                  _resident((SGU_GROUPS, CHUNK, CHUNK)), _resident((SGU_GROUPS, CHUNK, 1)),
                  _resident((1, SGU_WIDTH))],
        out_specs=[_rows(MIX_ROWS, SGU_WIDTH)] + [half_spec] * 3,
        out_shape=[jax.ShapeDtypeStruct((t, SGU_WIDTH), BF16), half, half, half],
        compiler_params=_params("parallel"),
        name="mix_in",
    )(h, pre_g, w_in, norm_g, norm_b, w_s, b_s, out_g)


def _sb_kernel(q_ref, k_ref, v_ref, o_ref, carry_ref):
    i = pl.program_id(2)
    lane = lax.broadcasted_iota(jnp.int32, (1, LANES), 1)
    first = lane < SB_HEAD_DIM

    r = lax.broadcasted_iota(jnp.int32, (SB_TILE, SB_TILE), 0)
    c = lax.broadcasted_iota(jnp.int32, (SB_TILE, SB_TILE), 1)
    suffix = jnp.concatenate([(r >= c).astype(BF16), jnp.ones((SB_TILE, SB_TILE), BF16)], axis=1)
    suffix = jnp.concatenate([suffix, suffix], axis=0)
    qrow = lax.broadcasted_iota(jnp.int32, (SB_ROWS, SB_TILE), 0)
    kcol = lax.broadcasted_iota(jnp.int32, (SB_ROWS, SB_TILE), 1)

    def rows(sub):
        return slice(sub * SB_ROWS, (sub + 1) * SB_ROWS)

    def q_heads(sub):
        q = q_ref[rows(sub), :]
        zero = jnp.zeros_like(q)
        return jnp.where(first, q, zero), jnp.where(first, zero, q)

    def scores(qs, start, n, valid):
        kw = k_ref[pl.ds(start, n * SB_TILE), :]
        zs, split = [], []
        for qh in qs:
            z = _dot_nt(qh, kw)
            for t in range(n):
                zt = z[:, t * SB_TILE:(t + 1) * SB_TILE]
                sp = jnp.maximum(zt, jnp.log(1.0 + jnp.exp(jnp.minimum(zt, SB_CLAMP))))
                if t == n - 1:
                    sp = jnp.where(valid, sp, 0.0)
                hi = sp.astype(BF16)
                lo = (sp - hi.astype(F32)).astype(BF16)
                zs.append(zt)
                split.append(jnp.concatenate([hi, lo], axis=1))
        return zs, jnp.concatenate(split, axis=0)

    def fold(zs, sums, start, n, valid, carry, acc):
        vw = v_ref[pl.ds(start, n * SB_TILE), :]
        weights, new_carry = [], []
        for hd, ch in enumerate(carry):
            parts = [None] * n
            for t in reversed(range(n)):
                blk = sums[(hd * n + t) * SB_ROWS:(hd * n + t + 1) * SB_ROWS]
                a = jnp.exp(zs[hd * n + t] - (ch + blk[:, :SB_TILE]))
                if t == n - 1:
                    a = jnp.where(valid, a, 0.0)
                parts[t] = a.astype(BF16)
                ch = ch + blk[:, SB_TILE:]
            weights.append(jnp.concatenate(parts, axis=1))
            new_carry.append(ch)
        vzero = jnp.zeros_like(vw)
        v_heads = jnp.concatenate([jnp.where(first, vw, vzero), jnp.where(first, vzero, vw)], axis=0)
        return new_carry, acc + _dot(jnp.concatenate(weights, axis=1), v_heads)

    def save(sub, carry, acc):
        carry_ref[sub, 0] = carry[0]
        carry_ref[sub, 1] = carry[1]
        o_ref[rows(sub), :] = acc

    def first_windows(windows):
        zeros = jnp.zeros((SB_ROWS, LANES), F32)
        masks = {lead: kcol < qrow + lead for lead in sorted({w[3] for w in windows})}
        staged = [scores(q_heads(sub), start, n, masks[lead]) for sub, n, start, lead in windows]
        sums = [_dot(split, suffix) for _, split in staged]
        for (sub, n, start, lead), (zs, _), sm in zip(windows, staged, sums):
            save(sub, *fold(zs, sm, start, n, masks[lead], (zeros, zeros), zeros))

    def static_window(sub):
        row0 = sub * SB_ROWS
        n = min(SB_WINDOW, -(-(row0 + SB_ROWS) // SB_TILE))
        start = max(row0 + SB_ROWS - SB_WINDOW * SB_TILE, 0)
        return sub, n, start, row0 - start - (n - 1) * SB_TILE

    @pl.when(i == 0)
    def _():
        first_windows([static_window(sub) for sub in range(SB_SUBS)])

    @pl.when(i > 0)
    def _():
        first_windows([(sub, SB_WINDOW,
                        pl.multiple_of((i * SB_SUBS + sub + 1) * SB_ROWS - SB_WINDOW * SB_TILE, SB_ROWS),
                        SB_TILE - SB_ROWS)
                       for sub in range(SB_SUBS)])

    lowest = carry_ref[0, 0]
    for sub in range(SB_SUBS):
        for hd in range(2):
            lowest = jnp.minimum(lowest, carry_ref[sub, hd])

    @pl.when(jnp.min(lowest) < SB_DONE)
    def _():
        for sub in range(SB_SUBS):
            qs = q_heads(sub)

            def cond(state):
                end, lowest_carry = state
                return jnp.logical_and(end > 0, lowest_carry < SB_DONE)

            def body(state, sub=sub, qs=qs):
                end, _ = state
                start = pl.multiple_of(jnp.maximum(end - SB_TILE, 0), SB_ROWS)
                valid = kcol < end - start
                zs, split = scores(qs, start, 1, valid)
                carry, acc = fold(zs, _dot(split, suffix), start, 1, valid,
                                  (carry_ref[sub, 0], carry_ref[sub, 1]), o_ref[rows(sub), :])
                save(sub, carry, acc)
                return start, jnp.min(jnp.minimum(carry[0], carry[1]))

            folded_from = jnp.maximum((i * SB_SUBS + sub + 1) * SB_ROWS - SB_WINDOW * SB_TILE, 0)
            lax.while_loop(cond, body, (folded_from,
                                        jnp.min(jnp.minimum(carry_ref[sub, 0], carry_ref[sub, 1]))))


def _sb_attention(q, k, v, batch, seq):
    pairs, t, _ = q.shape
    block = SB_SUBS * SB_ROWS
    nq = seq // block
    q_spec = pl.BlockSpec((None, block, LANES), lambda b, p, i: (p, b * nq + i, 0))
    kv_spec = pl.BlockSpec((None, seq, LANES), lambda b, p, i: (p, b, 0))
    return pl.pallas_call(
        _sb_kernel,
        grid=(batch, pairs, nq),
        in_specs=[q_spec, kv_spec, kv_spec],
        out_specs=q_spec,
        out_shape=jax.ShapeDtypeStruct((pairs, t, LANES), F32),
        scratch_shapes=[pltpu.VMEM((SB_SUBS, 2, SB_ROWS, LANES), F32)],
        compiler_params=_params("parallel", "parallel", "arbitrary"),
        name="sb_attn",
    )(q, k, v)


def _mem_kv_kernel(mem_ref, g_ref, w_ref, o_ref):
    o_ref[...] = _dot(_rms(mem_ref[...], g_ref[...]).astype(BF16), w_ref[...]).astype(BF16)


def _mem_kv(mem, g, w_kv):
    t, d = mem.shape
    n = w_kv.shape[1]
    return pl.pallas_call(
        _mem_kv_kernel,
        grid=(t // MEM_LEN,),
        in_specs=[_rows(MEM_LEN, d), _resident((1, d)), _resident((d, n))],
        out_specs=_rows(MEM_LEN, n),
        out_shape=jax.ShapeDtypeStruct((t, n), BF16),
        compiler_params=_params("parallel"),
        name="mem_kv",
    )(mem, g, w_kv)


def _post_mix_kernel(h_ref, oa_ref, ob_ref, sbg_ref, wout_ref, mpost_ref, xpre_ref, wq_ref,
                     kv_ref, wo_ref, xpost_ref, o_ref):
    tm = h_ref.shape[0] // POST_PARTS
    parts = [slice(p * tm, (p + 1) * tm) for p in range(POST_PARTS)]
    heads = [slice(hd * XA_HEAD_DIM, (hd + 1) * XA_HEAD_DIM) for hd in range(XA_HEADS)]

    merged = []
    for rs in parts:
        ob = jnp.concatenate([ob_ref[p, rs, :] for p in range(SB_PAIRS)], axis=1)
        ob = _rms(ob, sbg_ref[...]).astype(BF16)
        merged.append(_dot(oa_ref[rs, :], wout_ref[:SGU_WIDTH, :]) + _dot(ob, wout_ref[SGU_WIDTH:, :]))
    hs, qs = [], []
    for rs, m in zip(parts, merged):
        h = h_ref[rs, :] + _rms(m, mpost_ref[...])
        x = _rms(h, xpre_ref[...]).astype(BF16)
        hs.append(h)
        qs.append((_dot(x, wq_ref[...]) * (XA_HEAD_DIM ** -0.5)).astype(BF16))
    logits = [[_dot_nt(q[:, hd], kv_ref[:, hd]) for hd in heads] for q in qs]
    attended = []
    for part in logits:
        outs = []
        for hd, lg in zip(heads, part):
            e = jnp.exp(lg - jnp.max(lg, axis=-1, keepdims=True))
            o = _dot(e.astype(BF16), kv_ref[:, D_MODEL + hd.start:D_MODEL + hd.stop])
            outs.append((o / jnp.sum(e, axis=-1, keepdims=True)).astype(BF16))
        attended.append(jnp.concatenate(outs, axis=1))
    cs = [_dot(a, wo_ref[...]) for a in attended]
    for rs, h, c in zip(parts, hs, cs):
        o_ref[rs, :] = h + _rms(c, xpost_ref[...])


def _post_mix(h, out_a, out_b, sb_g, w_out, mix_post_g, xa_pre_g, w_q, kv, w_o, xa_post_g, seq):
    t, d = h.shape
    per_batch = seq // POST_ROWS
    vec = _resident((1, d))
    return pl.pallas_call(
        _post_mix_kernel,
        grid=(t // POST_ROWS,),
        in_specs=[_rows(POST_ROWS, d), _rows(POST_ROWS, SGU_WIDTH),
                  pl.BlockSpec((SB_PAIRS, POST_ROWS, LANES), lambda i: (0, i, 0)),
                  _resident((1, SB_WIDTH)), _resident((d, d)), vec, vec, _resident((d, d)),
                  pl.BlockSpec((MEM_LEN, 2 * d), lambda i: (i // per_batch, 0)),
                  _resident((d, d)), vec],
        out_specs=_rows(POST_ROWS, d),
        out_shape=jax.ShapeDtypeStruct((t, d), F32),
        compiler_params=_params("parallel"),
        name="post_mix",
    )(h, out_a, out_b, sb_g, w_out, mix_post_g, xa_pre_g, w_q, kv, w_o, xa_post_g)


def kernel(x, mem, ffn1_pre_g, ffn1_post_g, ffn1_w_gate, ffn1_w_up, ffn1_w_down, mix_pre_g, mix_post_g, w_in, sgu_norm_g, sgu_norm_b, sgu_w_s, sgu_b_s, sgu_out_g, sb_out_g, w_out, xa_pre_g, xa_post_g, mem_norm_g, xa_w_q, xa_w_kv, xa_w_o, ffn2_pre_g, ffn2_post_g, ffn2_w_gate, ffn2_w_up, ffn2_w_down, final_norm_g):
    batch, seq, d = x.shape
    depth = ffn1_pre_g.shape[0]
    h = x.reshape(batch * seq, d)
    mem2 = mem.reshape(batch * mem.shape[1], d)

    def w16(w):
        return w.astype(BF16)

    for l in range(depth):
        h = _ffn(h, ffn1_pre_g[l:l + 1], ffn1_post_g[l:l + 1],
                 w16(ffn1_w_gate[l]), w16(ffn1_w_up[l]), w16(ffn1_w_down[l]))
        out_a, q, k, v = _mix_in(h, mix_pre_g[l:l + 1], w16(w_in[l]), sgu_norm_g[l], sgu_norm_b[l],
                                 sgu_w_s[l], sgu_b_s[l][:, :, None], sgu_out_g[l:l + 1])
        out_b = _sb_attention(q, k, v, batch, seq)
        kv = _mem_kv(mem2, mem_norm_g[l:l + 1], w16(xa_w_kv[l]))
        h = _post_mix(h, out_a, out_b, sb_out_g[l:l + 1], w16(w_out[l]), mix_post_g[l:l + 1],
                      xa_pre_g[l:l + 1], w16(xa_w_q[l]), kv, w16(xa_w_o[l]), xa_post_g[l:l + 1], seq)
        h = _ffn(h, ffn2_pre_g[l:l + 1], ffn2_post_g[l:l + 1],
                 w16(ffn2_w_gate[l]), w16(ffn2_w_up[l]), w16(ffn2_w_down[l]),
                 final_g=final_norm_g[l:l + 1])
    return h.reshape(batch, seq, d)
```

```python
import functools

import jax
import jax.numpy as jnp
from jax import lax
from jax.experimental import pallas as pl
from jax.experimental.pallas import tpu as pltpu

D_MODEL = 1024
MEM_LEN = 256
SGU_WIDTH = 512
SGU_GROUPS = 4
SGU_GROUP_DIM = SGU_WIDTH // SGU_GROUPS
CHUNK = 128
SB_WIDTH = 512
SB_HEAD_DIM = 64
XA_HEADS = 4
XA_HEAD_DIM = D_MODEL // XA_HEADS
D_FF = 2816
IN_COLS = 2 * SGU_WIDTH + 3 * SB_WIDTH
EPS = 1e-6

LANES = 128
SB_PAIRS = SB_WIDTH // LANES
BF16_ROWS = 16
VMEM_LIMIT = 56 * 1024 * 1024

FFN_ROWS = 1024
FFN_PARTS = 4
MIX_ROWS = 1024
MIX_PARTS = 4
MEM_ROWS = 64
POST_ROWS = 1024
POST_PARTS = 4
SB_TILE = 128
SB_ROWS = 64
SB_SUBS = 16
SB_WINDOW = 2
SB_DONE = 105.0
SB_CLAMP = 80.0

F32 = jnp.float32
BF16 = jnp.bfloat16


def _rms(x, g):
    return x * lax.rsqrt(jnp.mean(x * x, axis=-1, keepdims=True) + EPS) * g


def _dot(a, b):
    return jnp.dot(a, b, preferred_element_type=F32)


def _dot_nt(a, b):
    return lax.dot_general(a, b, (((1,), (1,)), ((), ())), preferred_element_type=F32)


def _rows(tm, width):
    return pl.BlockSpec((tm, width), lambda i: (i, 0))


def _resident(shape):
    return pl.BlockSpec(shape, lambda *_: (0,) * len(shape), pipeline_mode=pl.Buffered(1))


def _params(*semantics):
    return pltpu.CompilerParams(dimension_semantics=semantics, vmem_limit_bytes=VMEM_LIMIT)


def _cast_spec(w, steps, step_of):
    rows, cols = w.shape
    n = max(c for c in range(1, steps + 1) if rows % c == 0 and (rows // c) % BF16_ROWS == 0)
    return pl.BlockSpec((rows // n, cols), lambda *idx: (jnp.minimum(step_of(*idx), n - 1), 0))


def _cast_chunks(w_refs, w16_refs):
    for w_ref, w16_ref in zip(w_refs, w16_refs):
        w16_ref[...] = w_ref[...].astype(BF16)


def _ffn_kernel(h_ref, pre_ref, post_ref, wg_ref, wu_ref, wd_ref, *rest, final_norm):
    o_ref = rest[-1]
    tm = h_ref.shape[0] // FFN_PARTS
    parts = [slice(p * tm, (p + 1) * tm) for p in range(FFN_PARTS)]

    def expand(rs):
        x = _rms(h_ref[rs, :], pre_ref[...]).astype(BF16)
        return _dot(x, wg_ref[...]), _dot(x, wu_ref[...])

    def contract(gate, up):
        return _dot((gate * jax.nn.sigmoid(gate) * up).astype(BF16), wd_ref[...])

    def finish(rs, f):
        y = h_ref[rs, :] + 0.5 * _rms(f, post_ref[...])
        if final_norm:
            y = _rms(y, rest[0][...])
        o_ref[rs, :] = y

    hidden = expand(parts[0])
    for p in range(FFN_PARTS):
        nxt = expand(parts[p + 1]) if p + 1 < FFN_PARTS else None
        f = contract(*hidden)
        if p > 0:
            finish(parts[p - 1], prev)
        prev, hidden = f, nxt
    finish(parts[-1], prev)


def _ffn(h, pre_g, post_g, w_gate, w_up, w_down, final_g=None):
    t, d = h.shape
    f = w_gate.shape[1]
    vec = _resident((1, d))
    in_specs = [_rows(FFN_ROWS, d), vec, vec, _resident((d, f)), _resident((d, f)), _resident((f, d))]
    args = [h, pre_g, post_g, w_gate, w_up, w_down]
    if final_g is not None:
        in_specs.append(vec)
        args.append(final_g)
    return pl.pallas_call(
        functools.partial(_ffn_kernel, final_norm=final_g is not None),
        grid=(t // FFN_ROWS,),
        in_specs=in_specs,
        out_specs=_rows(FFN_ROWS, d),
        out_shape=jax.ShapeDtypeStruct((t, d), F32),
        compiler_params=_params("parallel"),
        name="ffn_final" if final_g is not None else "ffn",
    )(*args)


def _mix_in_kernel(h_ref, g_ref, win_ref, ng_ref, nb_ref, ws_ref, bs_ref, og_ref,
                   oa_ref, q_ref, k_ref, v_ref):
    tm = h_ref.shape[0] // MIX_PARTS
    row = lax.broadcasted_iota(jnp.int32, (CHUNK, CHUNK), 0)
    col = lax.broadcasted_iota(jnp.int32, (CHUNK, CHUNK), 1)
    ws = [jnp.where(col <= row, ws_ref[g], 0.0).astype(BF16) for g in range(SGU_GROUPS)]

    def project(rs):
        return _dot(_rms(h_ref[rs, :], g_ref[...]).astype(BF16), win_ref[...])

    def gate(rs, proj):
        groups = []
        ss = jnp.zeros((tm, 1), F32)
        for g in range(SGU_GROUPS):
            lo = g * SGU_GROUP_DIM
            u = jax.nn.gelu(proj[:, lo:lo + SGU_GROUP_DIM])
            vg = jax.nn.gelu(proj[:, SGU_WIDTH + lo:SGU_WIDTH + lo + SGU_GROUP_DIM])
            mu = jnp.mean(vg, axis=-1, keepdims=True)
            dv = vg - mu
            var = jnp.mean(dv * dv, axis=-1, keepdims=True)
            vn = (dv * lax.rsqrt(var + EPS) * ng_ref[g:g + 1, :] + nb_ref[g:g + 1, :]).astype(BF16)
            mixed = jnp.concatenate(
                [_dot(ws[g], vn[c * CHUNK:(c + 1) * CHUNK, :]) + bs_ref[g] for c in range(tm // CHUNK)],
                axis=0)
            oa = u * mixed
            ss = ss + jnp.sum(oa * oa, axis=-1, keepdims=True)
            groups.append(oa)
        inv = lax.rsqrt(ss * (1.0 / SGU_WIDTH) + EPS)
        for g in range(SGU_GROUPS):
            lo = g * SGU_GROUP_DIM
            oa_ref[rs, lo:lo + SGU_GROUP_DIM] = (
                groups[g] * inv * og_ref[:, lo:lo + SGU_GROUP_DIM]).astype(BF16)
        base = 2 * SGU_WIDTH
        for p in range(SB_PAIRS):
            lo = base + p * LANES
            q_ref[p, rs, :] = (proj[:, lo:lo + LANES] * (SB_HEAD_DIM ** -0.5)).astype(BF16)
            k_ref[p, rs, :] = proj[:, lo + SB_WIDTH:lo + SB_WIDTH + LANES].astype(BF16)
            v_ref[p, rs, :] = proj[:, lo + 2 * SB_WIDTH:lo + 2 * SB_WIDTH + LANES].astype(BF16)

    parts = [slice(p * tm, (p + 1) * tm) for p in range(MIX_PARTS)]
    proj = project(parts[0])
    for p in range(1, MIX_PARTS):
        nxt = project(parts[p])
        gate(parts[p - 1], proj)
        proj = nxt
    gate(parts[-1], proj)


def _mix_in(h, pre_g, w_in, norm_g, norm_b, w_s, b_s, out_g):
    t, d = h.shape
    half = jax.ShapeDtypeStruct((SB_PAIRS, t, LANES), BF16)
    half_spec = pl.BlockSpec((SB_PAIRS, MIX_ROWS, LANES), lambda i: (0, i, 0))
    return pl.pallas_call(
        _mix_in_kernel,
        grid=(t // MIX_ROWS,),
        in_specs=[_rows(MIX_ROWS, d), _resident((1, d)), _resident((d, IN_COLS)),
                  _resident((SGU_GROUPS, SGU_GROUP_DIM)), _resident((SGU_GROUPS, SGU_GROUP_DIM)),
                  _resident((SGU_GROUPS, CHUNK, CHUNK)), _resident((SGU_GROUPS, CHUNK, 1)),
                  _resident((1, SGU_WIDTH))],
        out_specs=[_rows(MIX_ROWS, SGU_WIDTH)] + [half_spec] * 3,
        out_shape=[jax.ShapeDtypeStruct((t, SGU_WIDTH), BF16), half, half, half],
        compiler_params=_params("parallel"),
        name="mix_in",
    )(h, pre_g, w_in, norm_g, norm_b, w_s, b_s, out_g)


def _sb_kernel(q_ref, k_ref, v_ref, *refs, n_cast):
    _cast_chunks(refs[:n_cast], refs[n_cast + 1:-1])
    _sb_body(q_ref, k_ref, v_ref, refs[n_cast], refs[-1])


def _sb_body(q_ref, k_ref, v_ref, o_ref, carry_ref):
    i = pl.program_id(2)
    lane = lax.broadcasted_iota(jnp.int32, (1, LANES), 1)
    first = lane < SB_HEAD_DIM

    r = lax.broadcasted_iota(jnp.int32, (SB_TILE, SB_TILE), 0)
    c = lax.broadcasted_iota(jnp.int32, (SB_TILE, SB_TILE), 1)
    suffix = jnp.concatenate([(r >= c).astype(BF16), jnp.ones((SB_TILE, SB_TILE), BF16)], axis=1)
    suffix = jnp.concatenate([suffix, suffix], axis=0)
    qrow = lax.broadcasted_iota(jnp.int32, (SB_ROWS, SB_TILE), 0)
    kcol = lax.broadcasted_iota(jnp.int32, (SB_ROWS, SB_TILE), 1)

    def rows(sub):
        return slice(sub * SB_ROWS, (sub + 1) * SB_ROWS)

    def q_heads(sub):
        q = q_ref[rows(sub), :]
        zero = jnp.zeros_like(q)
        return jnp.where(first, q, zero), jnp.where(first, zero, q)

    def scores(qs, start, n, valid):
        kw = k_ref[pl.ds(start, n * SB_TILE), :]
        zs, split = [], []
        for qh in qs:
            z = _dot_nt(qh, kw)
            for t in range(n):
                zt = z[:, t * SB_TILE:(t + 1) * SB_TILE]
                sp = jnp.maximum(zt, jnp.log(1.0 + jnp.exp(jnp.minimum(zt, SB_CLAMP))))
                if t == n - 1:
                    sp = jnp.where(valid, sp, 0.0)
                hi = sp.astype(BF16)
                lo = (sp - hi.astype(F32)).astype(BF16)
                zs.append(zt)
                split.append(jnp.concatenate([hi, lo], axis=1))
        return zs, jnp.concatenate(split, axis=0)

    def fold(zs, sums, start, n, valid, carry, acc):
        vw = v_ref[pl.ds(start, n * SB_TILE), :]
        weights, new_carry = [], []
        for hd, ch in enumerate(carry):
            parts = [None] * n
            for t in reversed(range(n)):
                blk = sums[(hd * n + t) * SB_ROWS:(hd * n + t + 1) * SB_ROWS]
                a = jnp.exp(zs[hd * n + t] - (ch + blk[:, :SB_TILE]))
                if t == n - 1:
                    a = jnp.where(valid, a, 0.0)
                parts[t] = a.astype(BF16)
                ch = ch + blk[:, SB_TILE:]
            weights.append(jnp.concatenate(parts, axis=1))
            new_carry.append(ch)
        vzero = jnp.zeros_like(vw)
        v_heads = jnp.concatenate([jnp.where(first, vw, vzero), jnp.where(first, vzero, vw)], axis=0)
        return new_carry, acc + _dot(jnp.concatenate(weights, axis=1), v_heads)

    def save(sub, carry, acc):
        carry_ref[sub, 0] = carry[0]
        carry_ref[sub, 1] = carry[1]
        o_ref[rows(sub), :] = acc

    def first_windows(windows):
        zeros = jnp.zeros((SB_ROWS, LANES), F32)
        masks = {lead: kcol < qrow + lead for lead in sorted({w[3] for w in windows})}
        staged = [scores(q_heads(sub), start, n, masks[lead]) for sub, n, start, lead in windows]
        sums = [_dot(split, suffix) for _, split in staged]
        for (sub, n, start, lead), (zs, _), sm in zip(windows, staged, sums):
            save(sub, *fold(zs, sm, start, n, masks[lead], (zeros, zeros), zeros))

    def static_window(sub):
        row0 = sub * SB_ROWS
        n = min(SB_WINDOW, -(-(row0 + SB_ROWS) // SB_TILE))
        start = max(row0 + SB_ROWS - SB_WINDOW * SB_TILE, 0)
        return sub, n, start, row0 - start - (n - 1) * SB_TILE

    @pl.when(i == 0)
    def _():
        first_windows([static_window(sub) for sub in range(SB_SUBS)])

    @pl.when(i > 0)
    def _():
        first_windows([(sub, SB_WINDOW,
                        pl.multiple_of((i * SB_SUBS + sub + 1) * SB_ROWS - SB_WINDOW * SB_TILE, SB_ROWS),
                        SB_TILE - SB_ROWS)
                       for sub in range(SB_SUBS)])

    lowest = carry_ref[0, 0]
    for sub in range(SB_SUBS):
        for hd in range(2):
            lowest = jnp.minimum(lowest, carry_ref[sub, hd])

    @pl.when(jnp.min(lowest) < SB_DONE)
    def _():
        for sub in range(SB_SUBS):
            qs = q_heads(sub)

            def cond(state):
                end, lowest_carry = state
                return jnp.logical_and(end > 0, lowest_carry < SB_DONE)

            def body(state, sub=sub, qs=qs):
                end, _ = state
                start = pl.multiple_of(jnp.maximum(end - SB_TILE, 0), SB_ROWS)
                valid = kcol < end - start
                zs, split = scores(qs, start, 1, valid)
                carry, acc = fold(zs, _dot(split, suffix), start, 1, valid,
                                  (carry_ref[sub, 0], carry_ref[sub, 1]), o_ref[rows(sub), :])
                save(sub, carry, acc)
                return start, jnp.min(jnp.minimum(carry[0], carry[1]))

            folded_from = jnp.maximum((i * SB_SUBS + sub + 1) * SB_ROWS - SB_WINDOW * SB_TILE, 0)
            lax.while_loop(cond, body, (folded_from,
                                        jnp.min(jnp.minimum(carry_ref[sub, 0], carry_ref[sub, 1]))))


def _sb_attention(q, k, v, batch, seq, weights):
    pairs, t, _ = q.shape
    block = SB_SUBS * SB_ROWS
    nq = seq // block
    steps = batch * pairs * nq
    q_spec = pl.BlockSpec((None, block, LANES), lambda b, p, i: (p, b * nq + i, 0))
    kv_spec = pl.BlockSpec((None, seq, LANES), lambda b, p, i: (p, b, 0))
    cast_specs = [_cast_spec(w, steps, lambda b, p, i: (b * pairs + p) * nq + i) for w in weights]
    outs = pl.pallas_call(
        functools.partial(_sb_kernel, n_cast=len(weights)),
        grid=(batch, pairs, nq),
        in_specs=[q_spec, kv_spec, kv_spec] + cast_specs,
        out_specs=[q_spec] + cast_specs,
        out_shape=[jax.ShapeDtypeStruct((pairs, t, LANES), F32)]
                  + [jax.ShapeDtypeStruct(w.shape, BF16) for w in weights],
        scratch_shapes=[pltpu.VMEM((SB_SUBS, 2, SB_ROWS, LANES), F32)],
        compiler_params=_params("arbitrary", "arbitrary", "arbitrary"),
        name="sb_attn",
    )(q, k, v, *weights)
    return outs[0], outs[1:]


def _mem_kv_kernel(mem_ref, g_ref, w_ref, *refs, n_cast):
    o_ref, w16_ref = refs[n_cast], refs[-1]
    _cast_chunks(refs[:n_cast], refs[n_cast + 1:-1])

    @pl.when(pl.program_id(0) == 0)
    def _():
        w16_ref[...] = w_ref[...].astype(BF16)

    o_ref[...] = _dot(_rms(mem_ref[...], g_ref[...]).astype(BF16), w16_ref[...]).astype(BF16)


def _mem_kv(mem, g, w_kv, weights):
    t, d = mem.shape
    n = w_kv.shape[1]
    steps = t // MEM_ROWS
    cast_specs = [_cast_spec(w, steps, lambda i: i) for w in weights]
    outs = pl.pallas_call(
        functools.partial(_mem_kv_kernel, n_cast=len(weights)),
        grid=(steps,),
        in_specs=[_rows(MEM_ROWS, d), _resident((1, d)), _resident((d, n))] + cast_specs,
        out_specs=[_rows(MEM_ROWS, n)] + cast_specs,
        out_shape=[jax.ShapeDtypeStruct((t, n), BF16)]
                  + [jax.ShapeDtypeStruct(w.shape, BF16) for w in weights],
        scratch_shapes=[pltpu.VMEM((d, n), BF16)],
        compiler_params=_params("arbitrary"),
        name="mem_kv",
    )(mem, g, w_kv, *weights)
    return outs[0], outs[1:]


def _post_mix_kernel(h_ref, oa_ref, ob_ref, sbg_ref, wout_ref, mpost_ref, xpre_ref, wq_ref,
                     kv_ref, wo_ref, xpost_ref, o_ref):
    tm = h_ref.shape[0] // POST_PARTS
    parts = [slice(p * tm, (p + 1) * tm) for p in range(POST_PARTS)]
    heads = [slice(hd * XA_HEAD_DIM, (hd + 1) * XA_HEAD_DIM) for hd in range(XA_HEADS)]

    merged = []
    for rs in parts:
        ob = jnp.concatenate([ob_ref[p, rs, :] for p in range(SB_PAIRS)], axis=1)
        ob = _rms(ob, sbg_ref[...]).astype(BF16)
        merged.append(_dot(oa_ref[rs, :], wout_ref[:SGU_WIDTH, :]) + _dot(ob, wout_ref[SGU_WIDTH:, :]))
    hs, qs = [], []
    for rs, m in zip(parts, merged):
        h = h_ref[rs, :] + _rms(m, mpost_ref[...])
        x = _rms(h, xpre_ref[...]).astype(BF16)
        hs.append(h)
        qs.append((_dot(x, wq_ref[...]) * (XA_HEAD_DIM ** -0.5)).astype(BF16))
    logits = [[_dot_nt(q[:, hd], kv_ref[:, hd]) for hd in heads] for q in qs]
    attended = []
    for part in logits:
        outs = []
        for hd, lg in zip(heads, part):
            e = jnp.exp(lg - jnp.max(lg, axis=-1, keepdims=True))
            o = _dot(e.astype(BF16), kv_ref[:, D_MODEL + hd.start:D_MODEL + hd.stop])
            outs.append((o / jnp.sum(e, axis=-1, keepdims=True)).astype(BF16))
        attended.append(jnp.concatenate(outs, axis=1))
    cs = [_dot(a, wo_ref[...]) for a in attended]
    for rs, h, c in zip(parts, hs, cs):
        o_ref[rs, :] = h + _rms(c, xpost_ref[...])


def _post_mix(h, out_a, out_b, sb_g, w_out, mix_post_g, xa_pre_g, w_q, kv, w_o, xa_post_g, seq):
    t, d = h.shape
    per_batch = seq // POST_ROWS
    vec = _resident((1, d))
    return pl.pallas_call(
        _post_mix_kernel,
        grid=(t // POST_ROWS,),
        in_specs=[_rows(POST_ROWS, d), _rows(POST_ROWS, SGU_WIDTH),
                  pl.BlockSpec((SB_PAIRS, POST_ROWS, LANES), lambda i: (0, i, 0)),
                  _resident((1, SB_WIDTH)), _resident((d, d)), vec, vec, _resident((d, d)),
                  pl.BlockSpec((MEM_LEN, 2 * d), lambda i: (i // per_batch, 0)),
                  _resident((d, d)), vec],
        out_specs=_rows(POST_ROWS, d),
        out_shape=jax.ShapeDtypeStruct((t, d), F32),
        compiler_params=_params("parallel"),
        name="post_mix",
    )(h, out_a, out_b, sb_g, w_out, mix_post_g, xa_pre_g, w_q, kv, w_o, xa_post_g)


def kernel(x, mem, ffn1_pre_g, ffn1_post_g, ffn1_w_gate, ffn1_w_up, ffn1_w_down, mix_pre_g, mix_post_g, w_in, sgu_norm_g, sgu_norm_b, sgu_w_s, sgu_b_s, sgu_out_g, sb_out_g, w_out, xa_pre_g, xa_post_g, mem_norm_g, xa_w_q, xa_w_kv, xa_w_o, ffn2_pre_g, ffn2_post_g, ffn2_w_gate, ffn2_w_up, ffn2_w_down, final_norm_g):
    batch, seq, d = x.shape
    depth = ffn1_pre_g.shape[0]
    h = x.reshape(batch * seq, d)
    mem2 = mem.reshape(batch * mem.shape[1], d)

    for l in range(depth):
        kv, (gate16, up16, down16, w_in16) = _mem_kv(
            mem2, mem_norm_g[l:l + 1], xa_w_kv[l],
            [ffn1_w_gate[l], ffn1_w_up[l], ffn1_w_down[l], w_in[l]])
        h = _ffn(h, ffn1_pre_g[l:l + 1], ffn1_post_g[l:l + 1], gate16, up16, down16)
        out_a, q, k, v = _mix_in(h, mix_pre_g[l:l + 1], w_in16, sgu_norm_g[l], sgu_norm_b[l],
                                 sgu_w_s[l], sgu_b_s[l][:, :, None], sgu_out_g[l:l + 1])
        out_b, (w_out16, w_q16, w_o16, gate16, up16, down16) = _sb_attention(
            q, k, v, batch, seq,
            [w_out[l], xa_w_q[l], xa_w_o[l], ffn2_w_gate[l], ffn2_w_up[l], ffn2_w_down[l]])
        h = _post_mix(h, out_a, out_b, sb_out_g[l:l + 1], w_out16, mix_post_g[l:l + 1],
                      xa_pre_g[l:l + 1], w_q16, kv, w_o16, xa_post_g[l:l + 1], seq)
        h = _ffn(h, ffn2_pre_g[l:l + 1], ffn2_post_g[l:l + 1], gate16, up16, down16,
                 final_g=final_norm_g[l:l + 1])
    return h.reshape(batch, seq, d)
```

```python
import functools

import jax
import jax.numpy as jnp
from jax import lax
from jax.experimental import pallas as pl
from jax.experimental.pallas import tpu as pltpu

D_MODEL = 1024
MEM_LEN = 256
SGU_WIDTH = 512
SGU_GROUPS = 4
SGU_GROUP_DIM = SGU_WIDTH // SGU_GROUPS
CHUNK = 128
SB_WIDTH = 512
SB_HEAD_DIM = 64
XA_HEADS = 4
XA_HEAD_DIM = D_MODEL // XA_HEADS
D_FF = 2816
IN_COLS = 2 * SGU_WIDTH + 3 * SB_WIDTH
EPS = 1e-6

LANES = 128
SB_PAIRS = SB_WIDTH // LANES
BF16_ROWS = 16
VMEM_LIMIT = 56 * 1024 * 1024

FFN_ROWS = 1024
FFN_PARTS = 4
MIX_ROWS = 1024
MIX_PARTS = 4
MEM_ROWS = 64
POST_ROWS = 1024
POST_PARTS = 4
SB_TILE = 128
SB_ROWS = 64
SB_SUBS = 16
SB_WINDOW = 2
SB_DONE = 105.0
SB_CLAMP = 80.0

F32 = jnp.float32
BF16 = jnp.bfloat16


def _rms(x, g):
    return x * lax.rsqrt(jnp.mean(x * x, axis=-1, keepdims=True) + EPS) * g


def _dot(a, b):
    return jnp.dot(a, b, preferred_element_type=F32)


def _dot_nt(a, b):
    return lax.dot_general(a, b, (((1,), (1,)), ((), ())), preferred_element_type=F32)


def _rows(tm, width):
    return pl.BlockSpec((tm, width), lambda i: (i, 0))


def _resident(shape):
    return pl.BlockSpec(shape, lambda *_: (0,) * len(shape), pipeline_mode=pl.Buffered(1))


def _params(*semantics):
    return pltpu.CompilerParams(dimension_semantics=semantics, vmem_limit_bytes=VMEM_LIMIT)


def _cast_spec(w, steps, step_of):
    rows, cols = w.shape
    n = max(c for c in range(1, steps + 1) if rows % c == 0 and (rows // c) % BF16_ROWS == 0)
    return pl.BlockSpec((rows // n, cols), lambda *idx: (jnp.minimum(step_of(*idx), n - 1), 0))


def _call(body, name, grid, in_specs, out_specs, out_shape, args, casts=(), step_of=lambda i: i,
          scratch=()):
    steps = 1
    for g in grid:
        steps *= g
    n_in, n_out, n_cast = len(args), len(out_specs), len(casts)
    cast_specs = [_cast_spec(w, steps, step_of) for w in casts]

    def kernel(*refs):
        chunks = refs[n_in:n_in + n_cast]
        copies = refs[n_in + n_cast + n_out:n_in + 2 * n_cast + n_out]
        for w_ref, w16_ref in zip(chunks, copies):
            w16_ref[...] = w_ref[...].astype(BF16)
        body(*refs[:n_in], *refs[n_in + n_cast:n_in + n_cast + n_out], *refs[n_in + 2 * n_cast + n_out:])

    outs = pl.pallas_call(
        kernel,
        grid=grid,
        in_specs=list(in_specs) + cast_specs,
        out_specs=list(out_specs) + cast_specs,
        out_shape=list(out_shape) + [jax.ShapeDtypeStruct(w.shape, BF16) for w in casts],
        scratch_shapes=list(scratch),
        compiler_params=_params(*["arbitrary"] * len(grid)),
        name=name,
    )(*args, *casts)
    return outs[:n_out], outs[n_out:]


def _ffn_kernel(h_ref, pre_ref, post_ref, wg_ref, wu_ref, wd_ref, *rest, final_norm):
    o_ref = rest[-1]
    tm = h_ref.shape[0] // FFN_PARTS
    parts = [slice(p * tm, (p + 1) * tm) for p in range(FFN_PARTS)]

    def expand(rs):
        x = _rms(h_ref[rs, :], pre_ref[...]).astype(BF16)
        return _dot(x, wg_ref[...]), _dot(x, wu_ref[...])

    def contract(gate, up):
        return _dot((gate * jax.nn.sigmoid(gate) * up).astype(BF16), wd_ref[...])

    def finish(rs, f):
        y = h_ref[rs, :] + 0.5 * _rms(f, post_ref[...])
        if final_norm:
            y = _rms(y, rest[0][...])
        o_ref[rs, :] = y

    hidden = expand(parts[0])
    for p in range(FFN_PARTS):
        nxt = expand(parts[p + 1]) if p + 1 < FFN_PARTS else None
        f = contract(*hidden)
        if p > 0:
            finish(parts[p - 1], prev)
        prev, hidden = f, nxt
    finish(parts[-1], prev)


def _ffn(h, pre_g, post_g, w_gate, w_up, w_down, final_g=None, casts=()):
    t, d = h.shape
    f = w_gate.shape[1]
    vec = _resident((1, d))
    in_specs = [_rows(FFN_ROWS, d), vec, vec, _resident((d, f)), _resident((d, f)), _resident((f, d))]
    args = [h, pre_g, post_g, w_gate, w_up, w_down]
    if final_g is not None:
        in_specs.append(vec)
        args.append(final_g)
    (out,), copies = _call(
        functools.partial(_ffn_kernel, final_norm=final_g is not None),
        "ffn_final" if final_g is not None else "ffn",
        (t // FFN_ROWS,), in_specs, [_rows(FFN_ROWS, d)], [jax.ShapeDtypeStruct((t, d), F32)],
        args, casts)
    return out, copies


def _mix_in_kernel(h_ref, g_ref, win_ref, ng_ref, nb_ref, ws_ref, bs_ref, og_ref,
                   oa_ref, q_ref, k_ref, v_ref):
    tm = h_ref.shape[0] // MIX_PARTS
    row = lax.broadcasted_iota(jnp.int32, (CHUNK, CHUNK), 0)
    col = lax.broadcasted_iota(jnp.int32, (CHUNK, CHUNK), 1)
    ws = [jnp.where(col <= row, ws_ref[g], 0.0).astype(BF16) for g in range(SGU_GROUPS)]

    def project(rs):
        return _dot(_rms(h_ref[rs, :], g_ref[...]).astype(BF16), win_ref[...])

    def gate(rs, proj):
        groups = []
        ss = jnp.zeros((tm, 1), F32)
        for g in range(SGU_GROUPS):
            lo = g * SGU_GROUP_DIM
            u = jax.nn.gelu(proj[:, lo:lo + SGU_GROUP_DIM])
            vg = jax.nn.gelu(proj[:, SGU_WIDTH + lo:SGU_WIDTH + lo + SGU_GROUP_DIM])
            mu = jnp.mean(vg, axis=-1, keepdims=True)
            dv = vg - mu
            var = jnp.mean(dv * dv, axis=-1, keepdims=True)
            vn = (dv * lax.rsqrt(var + EPS) * ng_ref[g:g + 1, :] + nb_ref[g:g + 1, :]).astype(BF16)
            mixed = jnp.concatenate(
                [_dot(ws[g], vn[c * CHUNK:(c + 1) * CHUNK, :]) + bs_ref[g] for c in range(tm // CHUNK)],
                axis=0)
            oa = u * mixed
            ss = ss + jnp.sum(oa * oa, axis=-1, keepdims=True)
            groups.append(oa)
        inv = lax.rsqrt(ss * (1.0 / SGU_WIDTH) + EPS)
        for g in range(SGU_GROUPS):
            lo = g * SGU_GROUP_DIM
            oa_ref[rs, lo:lo + SGU_GROUP_DIM] = (
                groups[g] * inv * og_ref[:, lo:lo + SGU_GROUP_DIM]).astype(BF16)
        base = 2 * SGU_WIDTH
        for p in range(SB_PAIRS):
            lo = base + p * LANES
            q_ref[p, rs, :] = (proj[:, lo:lo + LANES] * (SB_HEAD_DIM ** -0.5)).astype(BF16)
            k_ref[p, rs, :] = proj[:, lo + SB_WIDTH:lo + SB_WIDTH + LANES].astype(BF16)
            v_ref[p, rs, :] = proj[:, lo + 2 * SB_WIDTH:lo + 2 * SB_WIDTH + LANES].astype(BF16)

    parts = [slice(p * tm, (p + 1) * tm) for p in range(MIX_PARTS)]
    proj = project(parts[0])
    for p in range(1, MIX_PARTS):
        nxt = project(parts[p])
        gate(parts[p - 1], proj)
        proj = nxt
    gate(parts[-1], proj)


def _mix_in(h, pre_g, w_in, norm_g, norm_b, w_s, b_s, out_g, casts=()):
    t, d = h.shape
    half = jax.ShapeDtypeStruct((SB_PAIRS, t, LANES), BF16)
    half_spec = pl.BlockSpec((SB_PAIRS, MIX_ROWS, LANES), lambda i: (0, i, 0))
    return _call(
        _mix_in_kernel, "mix_in", (t // MIX_ROWS,),
        [_rows(MIX_ROWS, d), _resident((1, d)), _resident((d, IN_COLS)),
         _resident((SGU_GROUPS, SGU_GROUP_DIM)), _resident((SGU_GROUPS, SGU_GROUP_DIM)),
         _resident((SGU_GROUPS, CHUNK, CHUNK)), _resident((SGU_GROUPS, CHUNK, 1)),
         _resident((1, SGU_WIDTH))],
        [_rows(MIX_ROWS, SGU_WIDTH)] + [half_spec] * 3,
        [jax.ShapeDtypeStruct((t, SGU_WIDTH), BF16), half, half, half],
        [h, pre_g, w_in, norm_g, norm_b, w_s, b_s, out_g], casts)


def _sb_kernel(q_ref, k_ref, v_ref, o_ref, carry_ref):
    i = pl.program_id(2)
    lane = lax.broadcasted_iota(jnp.int32, (1, LANES), 1)
    first = lane < SB_HEAD_DIM

    r = lax.broadcasted_iota(jnp.int32, (SB_TILE, SB_TILE), 0)
    c = lax.broadcasted_iota(jnp.int32, (SB_TILE, SB_TILE), 1)
    suffix = jnp.concatenate([(r >= c).astype(BF16), jnp.ones((SB_TILE, SB_TILE), BF16)], axis=1)
    suffix = jnp.concatenate([suffix, suffix], axis=0)
    qrow = lax.broadcasted_iota(jnp.int32, (SB_ROWS, SB_TILE), 0)
    kcol = lax.broadcasted_iota(jnp.int32, (SB_ROWS, SB_TILE), 1)

    def rows(sub):
        return slice(sub * SB_ROWS, (sub + 1) * SB_ROWS)

    def q_heads(sub):
        q = q_ref[rows(sub), :]
        zero = jnp.zeros_like(q)
        return jnp.where(first, q, zero), jnp.where(first, zero, q)

    def scores(qs, start, n, valid):
        kw = k_ref[pl.ds(start, n * SB_TILE), :]
        zs, split = [], []
        for qh in qs:
            z = _dot_nt(qh, kw)
            for t in range(n):
                zt = z[:, t * SB_TILE:(t + 1) * SB_TILE]
                sp = jnp.maximum(zt, jnp.log(1.0 + jnp.exp(jnp.minimum(zt, SB_CLAMP))))
                if t == n - 1:
                    sp = jnp.where(valid, sp, 0.0)
                hi = sp.astype(BF16)
                lo = (sp - hi.astype(F32)).astype(BF16)
                zs.append(zt)
                split.append(jnp.concatenate([hi, lo], axis=1))
        return zs, jnp.concatenate(split, axis=0)

    def fold(zs, sums, start, n, valid, carry, acc):
        vw = v_ref[pl.ds(start, n * SB_TILE), :]
        weights, new_carry = [], []
        for hd, ch in enumerate(carry):
            parts = [None] * n
            for t in reversed(range(n)):
                blk = sums[(hd * n + t) * SB_ROWS:(hd * n + t + 1) * SB_ROWS]
                a = jnp.exp(zs[hd * n + t] - (ch + blk[:, :SB_TILE]))
                if t == n - 1:
                    a = jnp.where(valid, a, 0.0)
                parts[t] = a.astype(BF16)
                ch = ch + blk[:, SB_TILE:]
            weights.append(jnp.concatenate(parts, axis=1))
            new_carry.append(ch)
        return new_carry, acc + jnp.where(first, _dot(weights[0], vw), _dot(weights[1], vw))

    def save(sub, carry, acc):
        carry_ref[sub, 0] = carry[0]
        carry_ref[sub, 1] = carry[1]
        o_ref[rows(sub), :] = acc

    def first_windows(windows):
        zeros = jnp.zeros((SB_ROWS, LANES), F32)
        masks = {lead: kcol < qrow + lead for lead in sorted({w[3] for w in windows})}
        staged = [scores(q_heads(sub), start, n, masks[lead]) for sub, n, start, lead in windows]
        sums = [_dot(split, suffix) for _, split in staged]
        for (sub, n, start, lead), (zs, _), sm in zip(windows, staged, sums):
            save(sub, *fold(zs, sm, start, n, masks[lead], (zeros, zeros), zeros))

    def static_window(sub):
        row0 = sub * SB_ROWS
        n = min(SB_WINDOW, -(-(row0 + SB_ROWS) // SB_TILE))
        start = max(row0 + SB_ROWS - SB_WINDOW * SB_TILE, 0)
        return sub, n, start, row0 - start - (n - 1) * SB_TILE

    @pl.when(i == 0)
    def _():
        first_windows([static_window(sub) for sub in range(SB_SUBS)])

    @pl.when(i > 0)
    def _():
        first_windows([(sub, SB_WINDOW,
                        pl.multiple_of((i * SB_SUBS + sub + 1) * SB_ROWS - SB_WINDOW * SB_TILE, SB_ROWS),
                        SB_TILE - SB_ROWS)
                       for sub in range(SB_SUBS)])

    lowest = carry_ref[0, 0]
    for sub in range(SB_SUBS):
        for hd in range(2):
            lowest = jnp.minimum(lowest, carry_ref[sub, hd])

    @pl.when(jnp.min(lowest) < SB_DONE)
    def _():
        for sub in range(SB_SUBS):
            qs = q_heads(sub)

            def cond(state):
                end, lowest_carry = state
                return jnp.logical_and(end > 0, lowest_carry < SB_DONE)

            def body(state, sub=sub, qs=qs):
                end, _ = state
                start = pl.multiple_of(jnp.maximum(end - SB_TILE, 0), SB_ROWS)
                valid = kcol < end - start
                zs, split = scores(qs, start, 1, valid)
                carry, acc = fold(zs, _dot(split, suffix), start, 1, valid,
                                  (carry_ref[sub, 0], carry_ref[sub, 1]), o_ref[rows(sub), :])
                save(sub, carry, acc)
                return start, jnp.min(jnp.minimum(carry[0], carry[1]))

            folded_from = jnp.maximum((i * SB_SUBS + sub + 1) * SB_ROWS - SB_WINDOW * SB_TILE, 0)
            lax.while_loop(cond, body, (folded_from,
                                        jnp.min(jnp.minimum(carry_ref[sub, 0], carry_ref[sub, 1]))))


def _sb_attention(q, k, v, batch, seq, casts=()):
    pairs, t, _ = q.shape
    block = SB_SUBS * SB_ROWS
    nq = seq // block
    q_spec = pl.BlockSpec((None, block, LANES), lambda b, p, i: (p, b * nq + i, 0))
    kv_spec = pl.BlockSpec((None, seq, LANES), lambda b, p, i: (p, b, 0))
    (out,), copies = _call(
        _sb_kernel, "sb_attn", (batch, pairs, nq), [q_spec, kv_spec, kv_spec], [q_spec],
        [jax.ShapeDtypeStruct((pairs, t, LANES), F32)], [q, k, v], casts,
        step_of=lambda b, p, i: (b * pairs + p) * nq + i,
        scratch=[pltpu.VMEM((SB_SUBS, 2, SB_ROWS, LANES), F32)])
    return out, copies


def _mem_kv_kernel(mem_ref, g_ref, w_ref, o_ref, w16_ref):
    @pl.when(pl.program_id(0) == 0)
    def _():
        w16_ref[...] = w_ref[...].astype(BF16)

    o_ref[...] = _dot(_rms(mem_ref[...], g_ref[...]).astype(BF16), w16_ref[...]).astype(BF16)


def _mem_kv(mem, g, w_kv, casts=()):
    t, d = mem.shape
    n = w_kv.shape[1]
    (kv,), copies = _call(
        _mem_kv_kernel, "mem_kv", (t // MEM_ROWS,),
        [_rows(MEM_ROWS, d), _resident((1, d)), _resident((d, n))], [_rows(MEM_ROWS, n)],
        [jax.ShapeDtypeStruct((t, n), BF16)], [mem, g, w_kv], casts,
        scratch=[pltpu.VMEM((d, n), BF16)])
    return kv, copies


def _post_mix_kernel(h_ref, oa_ref, ob_ref, sbg_ref, wout_ref, mpost_ref, xpre_ref, wq_ref,
                     kv_ref, wo_ref, xpost_ref, o_ref):
    tm = h_ref.shape[0] // POST_PARTS
    parts = [slice(p * tm, (p + 1) * tm) for p in range(POST_PARTS)]
    heads = [slice(hd * XA_HEAD_DIM, (hd + 1) * XA_HEAD_DIM) for hd in range(XA_HEADS)]

    merged = []
    for rs in parts:
        ob = jnp.concatenate([ob_ref[p, rs, :] for p in range(SB_PAIRS)], axis=1)
        ob = _rms(ob, sbg_ref[...]).astype(BF16)
        merged.append(_dot(oa_ref[rs, :], wout_ref[:SGU_WIDTH, :]) + _dot(ob, wout_ref[SGU_WIDTH:, :]))
    hs, qs = [], []
    for rs, m in zip(parts, merged):
        h = h_ref[rs, :] + _rms(m, mpost_ref[...])
        x = _rms(h, xpre_ref[...]).astype(BF16)
        hs.append(h)
        qs.append((_dot(x, wq_ref[...]) * (XA_HEAD_DIM ** -0.5)).astype(BF16))
    logits = [[_dot_nt(q[:, hd], kv_ref[:, hd]) for hd in heads] for q in qs]
    attended = []
    for part in logits:
        outs = []
        for hd, lg in zip(heads, part):
            e = jnp.exp(lg - jnp.max(lg, axis=-1, keepdims=True))
            o = _dot(e.astype(BF16), kv_ref[:, D_MODEL + hd.start:D_MODEL + hd.stop])
            outs.append((o / jnp.sum(e, axis=-1, keepdims=True)).astype(BF16))
        attended.append(jnp.concatenate(outs, axis=1))
    cs = [_dot(a, wo_ref[...]) for a in attended]
    for rs, h, c in zip(parts, hs, cs):
        o_ref[rs, :] = h + _rms(c, xpost_ref[...])


def _post_mix(h, out_a, out_b, sb_g, w_out, mix_post_g, xa_pre_g, w_q, kv, w_o, xa_post_g, seq):
    t, d = h.shape
    per_batch = seq // POST_ROWS
    vec = _resident((1, d))
    (out,), _ = _call(
        _post_mix_kernel, "post_mix", (t // POST_ROWS,),
        [_rows(POST_ROWS, d), _rows(POST_ROWS, SGU_WIDTH),
         pl.BlockSpec((SB_PAIRS, POST_ROWS, LANES), lambda i: (0, i, 0)),
         _resident((1, SB_WIDTH)), _resident((d, d)), vec, vec, _resident((d, d)),
         pl.BlockSpec((MEM_LEN, 2 * d), lambda i: (i // per_batch, 0)),
         _resident((d, d)), vec],
        [_rows(POST_ROWS, d)], [jax.ShapeDtypeStruct((t, d), F32)],
        [h, out_a, out_b, sb_g, w_out, mix_post_g, xa_pre_g, w_q, kv, w_o, xa_post_g])
    return out


def kernel(x, mem, ffn1_pre_g, ffn1_post_g, ffn1_w_gate, ffn1_w_up, ffn1_w_down, mix_pre_g, mix_post_g, w_in, sgu_norm_g, sgu_norm_b, sgu_w_s, sgu_b_s, sgu_out_g, sb_out_g, w_out, xa_pre_g, xa_post_g, mem_norm_g, xa_w_q, xa_w_kv, xa_w_o, ffn2_pre_g, ffn2_post_g, ffn2_w_gate, ffn2_w_up, ffn2_w_down, final_norm_g):
    batch, seq, d = x.shape
    depth = ffn1_pre_g.shape[0]
    h = x.reshape(batch * seq, d)
    mem2 = mem.reshape(batch * mem.shape[1], d)

    for l in range(depth):
        kv, (gate16, up16, down16) = _mem_kv(
            mem2, mem_norm_g[l:l + 1], xa_w_kv[l],
            casts=[ffn1_w_gate[l], ffn1_w_up[l], ffn1_w_down[l]])
        h, (w_in16,) = _ffn(h, ffn1_pre_g[l:l + 1], ffn1_post_g[l:l + 1], gate16, up16, down16,
                            casts=[w_in[l]])
        (out_a, q, k, v), (w_out16, w_q16, w_o16, gate16, up16, down16) = _mix_in(
            h, mix_pre_g[l:l + 1], w_in16, sgu_norm_g[l], sgu_norm_b[l],
            sgu_w_s[l], sgu_b_s[l][:, :, None], sgu_out_g[l:l + 1],
            casts=[w_out[l], xa_w_q[l], xa_w_o[l], ffn2_w_gate[l], ffn2_w_up[l], ffn2_w_down[l]])
        out_b, _ = _sb_attention(q, k, v, batch, seq)
        h = _post_mix(h, out_a, out_b, sb_out_g[l:l + 1], w_out16, mix_post_g[l:l + 1],
                      xa_pre_g[l:l + 1], w_q16, kv, w_o16, xa_post_g[l:l + 1], seq)
        h, _ = _ffn(h, ffn2_pre_g[l:l + 1], ffn2_post_g[l:l + 1], gate16, up16, down16,
                    final_g=final_norm_g[l:l + 1])
    return h.reshape(batch, seq, d)
```

```python
import functools

import jax
import jax.numpy as jnp
from jax import lax
from jax.experimental import pallas as pl
from jax.experimental.pallas import tpu as pltpu

D_MODEL = 1024
MEM_LEN = 256
SGU_WIDTH = 512
SGU_GROUPS = 4
SGU_GROUP_DIM = SGU_WIDTH // SGU_GROUPS
CHUNK = 128
SB_WIDTH = 512
SB_HEAD_DIM = 64
XA_HEADS = 4
XA_HEAD_DIM = D_MODEL // XA_HEADS
D_FF = 2816
IN_COLS = 2 * SGU_WIDTH + 3 * SB_WIDTH
EPS = 1e-6

LANES = 128
SB_PAIRS = SB_WIDTH // LANES
BF16_ROWS = 16
VMEM_LIMIT = 56 * 1024 * 1024

FFN_ROWS = 1024
FFN_PARTS = 4
MIX_ROWS = 1024
MIX_PARTS = 4
MEM_ROWS = 128
POST_ROWS = 1024
POST_PARTS = 4
SB_TILE = 128
SB_ROWS = 64
SB_SUBS = 16
SB_WINDOW = 2
SB_DONE = 105.0
SB_CLAMP = 80.0

F32 = jnp.float32
BF16 = jnp.bfloat16


def _rms(x, g):
    return x * lax.rsqrt(jnp.mean(x * x, axis=-1, keepdims=True) + EPS) * g


def _dot(a, b):
    return jnp.dot(a, b, preferred_element_type=F32)


def _dot_nt(a, b):
    return lax.dot_general(a, b, (((1,), (1,)), ((), ())), preferred_element_type=F32)


def _rows(tm, width):
    return pl.BlockSpec((tm, width), lambda i: (i, 0))


def _resident(shape):
    return pl.BlockSpec(shape, lambda *_: (0,) * len(shape), pipeline_mode=pl.Buffered(1))


def _params(*semantics):
    return pltpu.CompilerParams(dimension_semantics=semantics, vmem_limit_bytes=VMEM_LIMIT)


def _cast_spec(w, steps, step_of):
    rows, cols = w.shape
    n = max(c for c in range(1, steps + 1) if rows % c == 0 and (rows // c) % BF16_ROWS == 0)
    return pl.BlockSpec((rows // n, cols), lambda *idx: (jnp.minimum(step_of(*idx), n - 1), 0))


def _call(body, name, grid, in_specs, out_specs, out_shape, args, casts=(), step_of=lambda i: i,
          scratch=()):
    steps = 1
    for g in grid:
        steps *= g
    n_in, n_out, n_cast = len(args), len(out_specs), len(casts)
    cast_specs = [_cast_spec(w, steps, step_of) for w in casts]

    def kernel(*refs):
        chunks = refs[n_in:n_in + n_cast]
        copies = refs[n_in + n_cast + n_out:n_in + 2 * n_cast + n_out]
        for w_ref, w16_ref in zip(chunks, copies):
            w16_ref[...] = w_ref[...].astype(BF16)
        body(*refs[:n_in], *refs[n_in + n_cast:n_in + n_cast + n_out], *refs[n_in + 2 * n_cast + n_out:])

    outs = pl.pallas_call(
        kernel,
        grid=grid,
        in_specs=list(in_specs) + cast_specs,
        out_specs=list(out_specs) + cast_specs,
        out_shape=list(out_shape) + [jax.ShapeDtypeStruct(w.shape, BF16) for w in casts],
        scratch_shapes=list(scratch),
        compiler_params=_params(*["arbitrary"] * len(grid)),
        name=name,
    )(*args, *casts)
    return outs[:n_out], outs[n_out:]


def _ffn_kernel(h_ref, pre_ref, post_ref, wg_ref, wu_ref, wd_ref, *rest, final_norm):
    o_ref = rest[-1]
    tm = h_ref.shape[0] // FFN_PARTS
    parts = [slice(p * tm, (p + 1) * tm) for p in range(FFN_PARTS)]

    def expand(rs):
        x = _rms(h_ref[rs, :], pre_ref[...]).astype(BF16)
        return _dot(x, wg_ref[...]), _dot(x, wu_ref[...])

    def contract(gate, up):
        return _dot((gate * jax.nn.sigmoid(gate) * up).astype(BF16), wd_ref[...])

    def finish(rs, f):
        y = h_ref[rs, :] + 0.5 * _rms(f, post_ref[...])
        if final_norm:
            y = _rms(y, rest[0][...])
        o_ref[rs, :] = y

    hidden = expand(parts[0])
    for p in range(FFN_PARTS):
        nxt = expand(parts[p + 1]) if p + 1 < FFN_PARTS else None
        f = contract(*hidden)
        if p > 0:
            finish(parts[p - 1], prev)
        prev, hidden = f, nxt
    finish(parts[-1], prev)


def _ffn(h, pre_g, post_g, w_gate, w_up, w_down, final_g=None, casts=()):
    t, d = h.shape
    f = w_gate.shape[1]
    vec = _resident((1, d))
    in_specs = [_rows(FFN_ROWS, d), vec, vec, _resident((d, f)), _resident((d, f)), _resident((f, d))]
    args = [h, pre_g, post_g, w_gate, w_up, w_down]
    if final_g is not None:
        in_specs.append(vec)
        args.append(final_g)
    (out,), copies = _call(
        functools.partial(_ffn_kernel, final_norm=final_g is not None),
        "ffn_final" if final_g is not None else "ffn",
        (t // FFN_ROWS,), in_specs, [_rows(FFN_ROWS, d)], [jax.ShapeDtypeStruct((t, d), F32)],
        args, casts)
    return out, copies


def _mix_in_kernel(h_ref, g_ref, win_ref, ng_ref, nb_ref, ws_ref, bs_ref, og_ref,
                   oa_ref, q_ref, k_ref, v_ref):
    tm = h_ref.shape[0] // MIX_PARTS
    row = lax.broadcasted_iota(jnp.int32, (CHUNK, CHUNK), 0)
    col = lax.broadcasted_iota(jnp.int32, (CHUNK, CHUNK), 1)
    ws = [jnp.where(col <= row, ws_ref[g], 0.0).astype(BF16) for g in range(SGU_GROUPS)]

    def project(rs):
        return _dot(_rms(h_ref[rs, :], g_ref[...]).astype(BF16), win_ref[...])

    def gate(rs, proj):
        groups = []
        ss = jnp.zeros((tm, 1), F32)
        for g in range(SGU_GROUPS):
            lo = g * SGU_GROUP_DIM
            u = jax.nn.gelu(proj[:, lo:lo + SGU_GROUP_DIM])
            vg = jax.nn.gelu(proj[:, SGU_WIDTH + lo:SGU_WIDTH + lo + SGU_GROUP_DIM])
            mu = jnp.mean(vg, axis=-1, keepdims=True)
            dv = vg - mu
            var = jnp.mean(dv * dv, axis=-1, keepdims=True)
            vn = (dv * lax.rsqrt(var + EPS) * ng_ref[g:g + 1, :] + nb_ref[g:g + 1, :]).astype(BF16)
            chunks = [vn[c * CHUNK:(c + 1) * CHUNK, :] for c in range(tm // CHUNK)]
            wide = _dot(ws[g], jnp.concatenate(chunks, axis=1))
            mixed = jnp.concatenate(
                [wide[:, c * SGU_GROUP_DIM:(c + 1) * SGU_GROUP_DIM] + bs_ref[g]
                 for c in range(tm // CHUNK)], axis=0)
            oa = u * mixed
            ss = ss + jnp.sum(oa * oa, axis=-1, keepdims=True)
            groups.append(oa)
        inv = lax.rsqrt(ss * (1.0 / SGU_WIDTH) + EPS)
        for g in range(SGU_GROUPS):
            lo = g * SGU_GROUP_DIM
            oa_ref[rs, lo:lo + SGU_GROUP_DIM] = (
                groups[g] * inv * og_ref[:, lo:lo + SGU_GROUP_DIM]).astype(BF16)
        base = 2 * SGU_WIDTH
        for p in range(SB_PAIRS):
            lo = base + p * LANES
            q_ref[p, rs, :] = (proj[:, lo:lo + LANES] * (SB_HEAD_DIM ** -0.5)).astype(BF16)
            k_ref[p, rs, :] = proj[:, lo + SB_WIDTH:lo + SB_WIDTH + LANES].astype(BF16)
            v_ref[p, rs, :] = proj[:, lo + 2 * SB_WIDTH:lo + 2 * SB_WIDTH + LANES].astype(BF16)

    parts = [slice(p * tm, (p + 1) * tm) for p in range(MIX_PARTS)]
    proj = project(parts[0])
    for p in range(1, MIX_PARTS):
        nxt = project(parts[p])
        gate(parts[p - 1], proj)
        proj = nxt
    gate(parts[-1], proj)


def _mix_in(h, pre_g, w_in, norm_g, norm_b, w_s, b_s, out_g, casts=()):
    t, d = h.shape
    half = jax.ShapeDtypeStruct((SB_PAIRS, t, LANES), BF16)
    half_spec = pl.BlockSpec((SB_PAIRS, MIX_ROWS, LANES), lambda i: (0, i, 0))
    return _call(
        _mix_in_kernel, "mix_in", (t // MIX_ROWS,),
        [_rows(MIX_ROWS, d), _resident((1, d)), _resident((d, IN_COLS)),
         _resident((SGU_GROUPS, SGU_GROUP_DIM)), _resident((SGU_GROUPS, SGU_GROUP_DIM)),
         _resident((SGU_GROUPS, CHUNK, CHUNK)), _resident((SGU_GROUPS, CHUNK, 1)),
         _resident((1, SGU_WIDTH))],
        [_rows(MIX_ROWS, SGU_WIDTH)] + [half_spec] * 3,
        [jax.ShapeDtypeStruct((t, SGU_WIDTH), BF16), half, half, half],
        [h, pre_g, w_in, norm_g, norm_b, w_s, b_s, out_g], casts)


def _sb_kernel(q_ref, k_ref, v_ref, o_ref, carry_ref):
    i = pl.program_id(2)
    lane = lax.broadcasted_iota(jnp.int32, (1, LANES), 1)
    first = lane < SB_HEAD_DIM

    r = lax.broadcasted_iota(jnp.int32, (SB_TILE, SB_TILE), 0)
    c = lax.broadcasted_iota(jnp.int32, (SB_TILE, SB_TILE), 1)
    suffix = jnp.concatenate([(r >= c).astype(BF16), jnp.ones((SB_TILE, SB_TILE), BF16)], axis=1)
    suffix = jnp.concatenate([suffix, suffix], axis=0)
    qrow = lax.broadcasted_iota(jnp.int32, (SB_ROWS, SB_TILE), 0)
    kcol = lax.broadcasted_iota(jnp.int32, (SB_ROWS, SB_TILE), 1)

    def rows(sub):
        return slice(sub * SB_ROWS, (sub + 1) * SB_ROWS)

    def q_heads(sub):
        q = q_ref[rows(sub), :]
        zero = jnp.zeros_like(q)
        return jnp.where(first, q, zero), jnp.where(first, zero, q)

    def scores(qs, start, n, valid):
        kw = k_ref[pl.ds(start, n * SB_TILE), :]
        zs, split = [], []
        for qh in qs:
            z = _dot_nt(qh, kw)
            for t in range(n):
                zt = z[:, t * SB_TILE:(t + 1) * SB_TILE]
                sp = jnp.maximum(zt, jnp.log(1.0 + jnp.exp(jnp.minimum(zt, SB_CLAMP))))
                if t == n - 1:
                    sp = jnp.where(valid, sp, 0.0)
                hi = sp.astype(BF16)
                lo = (sp - hi.astype(F32)).astype(BF16)
                zs.append(zt)
                split.append(jnp.concatenate([hi, lo], axis=1))
        return zs, jnp.concatenate(split, axis=0)

    def fold(zs, sums, start, n, valid, carry, acc):
        vw = v_ref[pl.ds(start, n * SB_TILE), :]
        weights, new_carry = [], []
        for hd, ch in enumerate(carry):
            parts = [None] * n
            for t in reversed(range(n)):
                blk = sums[(hd * n + t) * SB_ROWS:(hd * n + t + 1) * SB_ROWS]
                a = jnp.exp(zs[hd * n + t] - (ch + blk[:, :SB_TILE]))
                if t == n - 1:
                    a = jnp.where(valid, a, 0.0)
                parts[t] = a.astype(BF16)
                ch = ch + blk[:, SB_TILE:]
            weights.append(jnp.concatenate(parts, axis=1))
            new_carry.append(ch)
        return new_carry, acc + jnp.where(first, _dot(weights[0], vw), _dot(weights[1], vw))

    def save(sub, carry, acc):
        carry_ref[sub, 0] = carry[0]
        carry_ref[sub, 1] = carry[1]
        o_ref[rows(sub), :] = acc

    def first_windows(windows):
        zeros = jnp.zeros((SB_ROWS, LANES), F32)
        masks = {lead: kcol < qrow + lead for lead in sorted({w[3] for w in windows})}
        staged = [scores(q_heads(sub), start, n, masks[lead]) for sub, n, start, lead in windows]
        sums = [_dot(split, suffix) for _, split in staged]
        for (sub, n, start, lead), (zs, _), sm in zip(windows, staged, sums):
            save(sub, *fold(zs, sm, start, n, masks[lead], (zeros, zeros), zeros))

    def static_window(sub):
        row0 = sub * SB_ROWS
        n = min(SB_WINDOW, -(-(row0 + SB_ROWS) // SB_TILE))
        start = max(row0 + SB_ROWS - SB_WINDOW * SB_TILE, 0)
        return sub, n, start, row0 - start - (n - 1) * SB_TILE

    @pl.when(i == 0)
    def _():
        first_windows([static_window(sub) for sub in range(SB_SUBS)])

    @pl.when(i > 0)
    def _():
        first_windows([(sub, SB_WINDOW,
                        pl.multiple_of((i * SB_SUBS + sub + 1) * SB_ROWS - SB_WINDOW * SB_TILE, SB_ROWS),
                        SB_TILE - SB_ROWS)
                       for sub in range(SB_SUBS)])

    lowest = carry_ref[0, 0]
    for sub in range(SB_SUBS):
        for hd in range(2):
            lowest = jnp.minimum(lowest, carry_ref[sub, hd])

    @pl.when(jnp.min(lowest) < SB_DONE)
    def _():
        for sub in range(SB_SUBS):
            qs = q_heads(sub)

            def cond(state):
                end, lowest_carry = state
                return jnp.logical_and(end > 0, lowest_carry < SB_DONE)

            def body(state, sub=sub, qs=qs):
                end, _ = state
                start = pl.multiple_of(jnp.maximum(end - SB_TILE, 0), SB_ROWS)
                valid = kcol < end - start
                zs, split = scores(qs, start, 1, valid)
                carry, acc = fold(zs, _dot(split, suffix), start, 1, valid,
                                  (carry_ref[sub, 0], carry_ref[sub, 1]), o_ref[rows(sub), :])
                save(sub, carry, acc)
                return start, jnp.min(jnp.minimum(carry[0], carry[1]))

            folded_from = jnp.maximum((i * SB_SUBS + sub + 1) * SB_ROWS - SB_WINDOW * SB_TILE, 0)
            lax.while_loop(cond, body, (folded_from,
                                        jnp.min(jnp.minimum(carry_ref[sub, 0], carry_ref[sub, 1]))))


def _sb_attention(q, k, v, batch, seq, casts=()):
    pairs, t, _ = q.shape
    block = SB_SUBS * SB_ROWS
    nq = seq // block
    q_spec = pl.BlockSpec((None, block, LANES), lambda b, p, i: (p, b * nq + i, 0))
    kv_spec = pl.BlockSpec((None, seq, LANES), lambda b, p, i: (p, b, 0))
    (out,), copies = _call(
        _sb_kernel, "sb_attn", (batch, pairs, nq), [q_spec, kv_spec, kv_spec], [q_spec],
        [jax.ShapeDtypeStruct((pairs, t, LANES), F32)], [q, k, v], casts,
        step_of=lambda b, p, i: (b * pairs + p) * nq + i,
        scratch=[pltpu.VMEM((SB_SUBS, 2, SB_ROWS, LANES), F32)])
    return out, copies


def _mem_kv_kernel(mem_ref, g_ref, w_ref, o_ref, w16_ref):
    @pl.when(pl.program_id(0) == 0)
    def _():
        w16_ref[...] = w_ref[...].astype(BF16)

    o_ref[...] = _dot(_rms(mem_ref[...], g_ref[...]).astype(BF16), w16_ref[...]).astype(BF16)


def _mem_kv(mem, g, w_kv, casts=()):
    t, d = mem.shape
    n = w_kv.shape[1]
    (kv,), copies = _call(
        _mem_kv_kernel, "mem_kv", (t // MEM_ROWS,),
        [_rows(MEM_ROWS, d), _resident((1, d)), _resident((d, n))], [_rows(MEM_ROWS, n)],
        [jax.ShapeDtypeStruct((t, n), BF16)], [mem, g, w_kv], casts,
        scratch=[pltpu.VMEM((d, n), BF16)])
    return kv, copies


def _post_mix_kernel(h_ref, oa_ref, ob_ref, sbg_ref, wout_ref, mpost_ref, xpre_ref, wq_ref,
                     kv_ref, wo_ref, xpost_ref, o_ref):
    tm = h_ref.shape[0] // POST_PARTS
    parts = [slice(p * tm, (p + 1) * tm) for p in range(POST_PARTS)]
    heads = [slice(hd * XA_HEAD_DIM, (hd + 1) * XA_HEAD_DIM) for hd in range(XA_HEADS)]

    def merge(rs, st):
        ob = jnp.concatenate([ob_ref[p, rs, :] for p in range(SB_PAIRS)], axis=1)
        ob = _rms(ob, sbg_ref[...]).astype(BF16)
        st["merged"] = _dot(oa_ref[rs, :], wout_ref[:SGU_WIDTH, :]) + _dot(ob, wout_ref[SGU_WIDTH:, :])

    def query(rs, st):
        st["h"] = h_ref[rs, :] + _rms(st.pop("merged"), mpost_ref[...])
        x = _rms(st["h"], xpre_ref[...]).astype(BF16)
        st["q"] = (_dot(x, wq_ref[...]) * (XA_HEAD_DIM ** -0.5)).astype(BF16)

    def score(rs, st):
        q = st.pop("q")
        st["logits"] = [_dot_nt(q[:, hd], kv_ref[:, hd]) for hd in heads]
    def attend(rs, st):
        outs = []
        for hd, lg in zip(heads, st.pop("logits")):
            e = jnp.exp(lg - jnp.max(lg, axis=-1, keepdims=True))
            o = _dot(e.astype(BF16), kv_ref[:, D_MODEL + hd.start:D_MODEL + hd.stop])
            outs.append((o / jnp.sum(e, axis=-1, keepdims=True)).astype(BF16))
        st["attended"] = jnp.concatenate(outs, axis=1)

    def project(rs, st):
        st["c"] = _dot(st.pop("attended"), wo_ref[...])

    def finish(rs, st):
        o_ref[rs, :] = st.pop("h") + _rms(st.pop("c"), xpost_ref[...])

    stages = [merge, query, score, attend, project, finish]
    states = [{} for _ in parts]
    for slot in range(len(parts) + len(stages) - 1):
        for s, stage in enumerate(stages):
            p = slot - s
            if 0 <= p < len(parts):
                stage(parts[p], states[p])


def _post_mix(h, out_a, out_b, sb_g, w_out, mix_post_g, xa_pre_g, w_q, kv, w_o, xa_post_g, seq):
    t, d = h.shape
    per_batch = seq // POST_ROWS
    vec = _resident((1, d))
    (out,), _ = _call(
        _post_mix_kernel, "post_mix", (t // POST_ROWS,),
        [_rows(POST_ROWS, d), _rows(POST_ROWS, SGU_WIDTH),
         pl.BlockSpec((SB_PAIRS, POST_ROWS, LANES), lambda i: (0, i, 0)),
         _resident((1, SB_WIDTH)), _resident((d, d)), vec, vec, _resident((d, d)),
         pl.BlockSpec((MEM_LEN, 2 * d), lambda i: (i // per_batch, 0)),
         _resident((d, d)), vec],
        [_rows(POST_ROWS, d)], [jax.ShapeDtypeStruct((t, d), F32)],
        [h, out_a, out_b, sb_g, w_out, mix_post_g, xa_pre_g, w_q, kv, w_o, xa_post_g])
    return out


def kernel(x, mem, ffn1_pre_g, ffn1_post_g, ffn1_w_gate, ffn1_w_up, ffn1_w_down, mix_pre_g, mix_post_g, w_in, sgu_norm_g, sgu_norm_b, sgu_w_s, sgu_b_s, sgu_out_g, sb_out_g, w_out, xa_pre_g, xa_post_g, mem_norm_g, xa_w_q, xa_w_kv, xa_w_o, ffn2_pre_g, ffn2_post_g, ffn2_w_gate, ffn2_w_up, ffn2_w_down, final_norm_g):
    batch, seq, d = x.shape
    depth = ffn1_pre_g.shape[0]
    h = x.reshape(batch * seq, d)
    mem2 = mem.reshape(batch * mem.shape[1], d)

    for l in range(depth):
        kv, (gate16, up16, down16) = _mem_kv(
            mem2, mem_norm_g[l:l + 1], xa_w_kv[l],
            casts=[ffn1_w_gate[l], ffn1_w_up[l], ffn1_w_down[l]])
        h, (w_in16,) = _ffn(h, ffn1_pre_g[l:l + 1], ffn1_post_g[l:l + 1], gate16, up16, down16,
                            casts=[w_in[l]])
        (out_a, q, k, v), (w_out16, w_q16, w_o16, gate16, up16, down16) = _mix_in(
            h, mix_pre_g[l:l + 1], w_in16, sgu_norm_g[l], sgu_norm_b[l],
            sgu_w_s[l], sgu_b_s[l][:, :, None], sgu_out_g[l:l + 1],
            casts=[w_out[l], xa_w_q[l], xa_w_o[l], ffn2_w_gate[l], ffn2_w_up[l], ffn2_w_down[l]])
        out_b, _ = _sb_attention(q, k, v, batch, seq)
        h = _post_mix(h, out_a, out_b, sb_out_g[l:l + 1], w_out16, mix_post_g[l:l + 1],
                      xa_pre_g[l:l + 1], w_q16, kv, w_o16, xa_post_g[l:l + 1], seq)
        h, _ = _ffn(h, ffn2_pre_g[l:l + 1], ffn2_post_g[l:l + 1], gate16, up16, down16,
                    final_g=final_norm_g[l:l + 1])
    return h.reshape(batch, seq, d)
```

```python
import functools

import jax
import jax.numpy as jnp
from jax import lax
from jax.experimental import pallas as pl
from jax.experimental.pallas import tpu as pltpu

D_MODEL = 1024
MEM_LEN = 256
SGU_WIDTH = 512
SGU_GROUPS = 4
SGU_GROUP_DIM = SGU_WIDTH // SGU_GROUPS
CHUNK = 128
SB_WIDTH = 512
SB_HEAD_DIM = 64
XA_HEADS = 4
XA_HEAD_DIM = D_MODEL // XA_HEADS
D_FF = 2816
IN_COLS = 2 * SGU_WIDTH + 3 * SB_WIDTH
EPS = 1e-6

LANES = 128
SB_PAIRS = SB_WIDTH // LANES
BF16_ROWS = 16
VMEM_LIMIT = 56 * 1024 * 1024

FFN_ROWS = 1024
FFN_PARTS = 4
FFN_TILE = 256
MIX_ROWS = 1024
MIX_PARTS = 4
MEM_ROWS = 256
POST_ROWS = 1024
POST_PARTS = 4
SB_TILE = 128
SB_ROWS = 64
SB_SUBS = 16
SB_WINDOW = 2
SB_DONE = 105.0
SB_CLAMP = 80.0

F32 = jnp.float32
BF16 = jnp.bfloat16


def _rms(x, g):
    return x * lax.rsqrt(jnp.mean(x * x, axis=-1, keepdims=True) + EPS) * g


def _dot(a, b):
    return jnp.dot(a, b, preferred_element_type=F32)


def _dot_nt(a, b):
    return lax.dot_general(a, b, (((1,), (1,)), ((), ())), preferred_element_type=F32)


def _rows(tm, width):
    return pl.BlockSpec((tm, width), lambda i: (i, 0))


def _resident(shape):
    return pl.BlockSpec(shape, lambda *_: (0,) * len(shape), pipeline_mode=pl.Buffered(1))


def _params(*semantics):
    return pltpu.CompilerParams(dimension_semantics=semantics, vmem_limit_bytes=VMEM_LIMIT)


def _cast_spec(w, steps, step_of):
    rows, cols = w.shape
    n = max(c for c in range(1, steps + 1) if rows % c == 0 and (rows // c) % BF16_ROWS == 0)
    return pl.BlockSpec((rows // n, cols), lambda *idx: (jnp.minimum(step_of(*idx), n - 1), 0))


def _call(body, name, grid, in_specs, out_specs, out_shape, args, casts=(), step_of=lambda i: i,
          scratch=()):
    steps = 1
    for g in grid:
        steps *= g
    n_in, n_out, n_cast = len(args), len(out_specs), len(casts)
    cast_specs = [_cast_spec(w, steps, step_of) for w in casts]

    def kernel(*refs):
        chunks = refs[n_in:n_in + n_cast]
        copies = refs[n_in + n_cast + n_out:n_in + 2 * n_cast + n_out]
        for w_ref, w16_ref in zip(chunks, copies):
            w16_ref[...] = w_ref[...].astype(BF16)
        body(*refs[:n_in], *refs[n_in + n_cast:n_in + n_cast + n_out], *refs[n_in + 2 * n_cast + n_out:])

    outs = pl.pallas_call(
        kernel,
        grid=grid,
        in_specs=list(in_specs) + cast_specs,
        out_specs=list(out_specs) + cast_specs,
        out_shape=list(out_shape) + [jax.ShapeDtypeStruct(w.shape, BF16) for w in casts],
        scratch_shapes=list(scratch),
        compiler_params=_params(*["arbitrary"] * len(grid)),
        name=name,
    )(*args, *casts)
    return outs[:n_out], outs[n_out:]


def _ffn_kernel(h_ref, pre_ref, post_ref, wg_ref, wu_ref, wd_ref, *rest, final_norm):
    o_ref = rest[-1]
    tm = h_ref.shape[0] // FFN_PARTS
    parts = [slice(p * tm, (p + 1) * tm) for p in range(FFN_PARTS)]

    def expand(rs):
        x = _rms(h_ref[rs, :], pre_ref[...]).astype(BF16)
        act = []
        for lo in range(0, wg_ref.shape[1], FFN_TILE):
            gate = _dot(x, wg_ref[:, lo:lo + FFN_TILE])
            up = _dot(x, wu_ref[:, lo:lo + FFN_TILE])
            act.append((gate * jax.nn.sigmoid(gate) * up).astype(BF16))
        return jnp.concatenate(act, axis=1)

    def contract(act):
        return _dot(act, wd_ref[...])

    def finish(rs, f):
        y = h_ref[rs, :] + 0.5 * _rms(f, post_ref[...])
        if final_norm:
            y = _rms(y, rest[0][...])
        o_ref[rs, :] = y

    hidden = expand(parts[0])
    for p in range(FFN_PARTS):
        nxt = expand(parts[p + 1]) if p + 1 < FFN_PARTS else None
        f = contract(hidden)
        if p > 0:
            finish(parts[p - 1], prev)
        prev, hidden = f, nxt
    finish(parts[-1], prev)


def _ffn(h, pre_g, post_g, w_gate, w_up, w_down, final_g=None, casts=()):
    t, d = h.shape
    f = w_gate.shape[1]
    vec = _resident((1, d))
    in_specs = [_rows(FFN_ROWS, d), vec, vec, _resident((d, f)), _resident((d, f)), _resident((f, d))]
    args = [h, pre_g, post_g, w_gate, w_up, w_down]
    if final_g is not None:
        in_specs.append(vec)
        args.append(final_g)
    (out,), copies = _call(
        functools.partial(_ffn_kernel, final_norm=final_g is not None),
        "ffn_final" if final_g is not None else "ffn",
        (t // FFN_ROWS,), in_specs, [_rows(FFN_ROWS, d)], [jax.ShapeDtypeStruct((t, d), F32)],
        args, casts)
    return out, copies


def _mix_in_kernel(h_ref, g_ref, win_ref, ng_ref, nb_ref, ws_ref, bs_ref, og_ref,
                   oa_ref, q_ref, k_ref, v_ref):
    tm = h_ref.shape[0] // MIX_PARTS
    row = lax.broadcasted_iota(jnp.int32, (CHUNK, CHUNK), 0)
    col = lax.broadcasted_iota(jnp.int32, (CHUNK, CHUNK), 1)
    ws = [jnp.where(col <= row, ws_ref[g], 0.0).astype(BF16) for g in range(SGU_GROUPS)]

    def project(rs):
        return _dot(_rms(h_ref[rs, :], g_ref[...]).astype(BF16), win_ref[...])

    def gate(rs, proj):
        groups = []
        ss = jnp.zeros((tm, 1), F32)
        for g in range(SGU_GROUPS):
            lo = g * SGU_GROUP_DIM
            u = jax.nn.gelu(proj[:, lo:lo + SGU_GROUP_DIM])
            vg = jax.nn.gelu(proj[:, SGU_WIDTH + lo:SGU_WIDTH + lo + SGU_GROUP_DIM])
            mu = jnp.mean(vg, axis=-1, keepdims=True)
            dv = vg - mu
            var = jnp.mean(dv * dv, axis=-1, keepdims=True)
            vn = (dv * lax.rsqrt(var + EPS) * ng_ref[g:g + 1, :] + nb_ref[g:g + 1, :]).astype(BF16)
            chunks = [vn[c * CHUNK:(c + 1) * CHUNK, :] for c in range(tm // CHUNK)]
            wide = _dot(ws[g], jnp.concatenate(chunks, axis=1))
            mixed = jnp.concatenate(
                [wide[:, c * SGU_GROUP_DIM:(c + 1) * SGU_GROUP_DIM] + bs_ref[g]
                 for c in range(tm // CHUNK)], axis=0)
            oa = u * mixed
            ss = ss + jnp.sum(oa * oa, axis=-1, keepdims=True)
            groups.append(oa)
        inv = lax.rsqrt(ss * (1.0 / SGU_WIDTH) + EPS)
        for g in range(SGU_GROUPS):
            lo = g * SGU_GROUP_DIM
            oa_ref[rs, lo:lo + SGU_GROUP_DIM] = (
                groups[g] * inv * og_ref[:, lo:lo + SGU_GROUP_DIM]).astype(BF16)
        base = 2 * SGU_WIDTH
        for p in range(SB_PAIRS):
            lo = base + p * LANES
            q_ref[p, rs, :] = (proj[:, lo:lo + LANES] * (SB_HEAD_DIM ** -0.5)).astype(BF16)
            k_ref[p, rs, :] = proj[:, lo + SB_WIDTH:lo + SB_WIDTH + LANES].astype(BF16)
            v_ref[p, rs, :] = proj[:, lo + 2 * SB_WIDTH:lo + 2 * SB_WIDTH + LANES].astype(BF16)

    parts = [slice(p * tm, (p + 1) * tm) for p in range(MIX_PARTS)]
    proj = project(parts[0])
    for p in range(1, MIX_PARTS):
        nxt = project(parts[p])
        gate(parts[p - 1], proj)
        proj = nxt
    gate(parts[-1], proj)


def _mix_in(h, pre_g, w_in, norm_g, norm_b, w_s, b_s, out_g, casts=()):
    t, d = h.shape
    half = jax.ShapeDtypeStruct((SB_PAIRS, t, LANES), BF16)
    half_spec = pl.BlockSpec((SB_PAIRS, MIX_ROWS, LANES), lambda i: (0, i, 0))
    return _call(
        _mix_in_kernel, "mix_in", (t // MIX_ROWS,),
        [_rows(MIX_ROWS, d), _resident((1, d)), _resident((d, IN_COLS)),
         _resident((SGU_GROUPS, SGU_GROUP_DIM)), _resident((SGU_GROUPS, SGU_GROUP_DIM)),
         _resident((SGU_GROUPS, CHUNK, CHUNK)), _resident((SGU_GROUPS, CHUNK, 1)),
         _resident((1, SGU_WIDTH))],
        [_rows(MIX_ROWS, SGU_WIDTH)] + [half_spec] * 3,
        [jax.ShapeDtypeStruct((t, SGU_WIDTH), BF16), half, half, half],
        [h, pre_g, w_in, norm_g, norm_b, w_s, b_s, out_g], casts)


def _sb_kernel(q_ref, k_ref, v_ref, o_ref, carry_ref):
    i = pl.program_id(2)
    lane = lax.broadcasted_iota(jnp.int32, (1, LANES), 1)
    first = lane < SB_HEAD_DIM

    r = lax.broadcasted_iota(jnp.int32, (SB_TILE, SB_TILE), 0)
    c = lax.broadcasted_iota(jnp.int32, (SB_TILE, SB_TILE), 1)
    suffix = jnp.concatenate([(r >= c).astype(BF16), jnp.ones((SB_TILE, SB_TILE), BF16)], axis=1)
    suffix = jnp.concatenate([suffix, suffix], axis=0)
    qrow = lax.broadcasted_iota(jnp.int32, (SB_ROWS, SB_TILE), 0)
    kcol = lax.broadcasted_iota(jnp.int32, (SB_ROWS, SB_TILE), 1)

    def rows(sub):
        return slice(sub * SB_ROWS, (sub + 1) * SB_ROWS)

    def q_heads(sub):
        q = q_ref[rows(sub), :]
        zero = jnp.zeros_like(q)
        return jnp.where(first, q, zero), jnp.where(first, zero, q)

    def scores(qs, start, n, valid):
        kw = k_ref[pl.ds(start, n * SB_TILE), :]
        zs, split = [], []
        for qh in qs:
            z = _dot_nt(qh, kw)
            for t in range(n):
                zt = z[:, t * SB_TILE:(t + 1) * SB_TILE]
                sp = jnp.maximum(zt, jnp.log(1.0 + jnp.exp(jnp.minimum(zt, SB_CLAMP))))
                if t == n - 1:
                    sp = jnp.where(valid, sp, 0.0)
                hi = sp.astype(BF16)
                lo = (sp - hi.astype(F32)).astype(BF16)
                zs.append(zt)
                split.append(jnp.concatenate([hi, lo], axis=1))
        return zs, jnp.concatenate(split, axis=0)

    def fold(zs, sums, start, n, valid, carry, acc):
        vw = v_ref[pl.ds(start, n * SB_TILE), :]
        weights, new_carry = [], []
        for hd, ch in enumerate(carry):
            parts = [None] * n
            for t in reversed(range(n)):
                blk = sums[(hd * n + t) * SB_ROWS:(hd * n + t + 1) * SB_ROWS]
                a = jnp.exp(zs[hd * n + t] - (ch + blk[:, :SB_TILE]))
                if t == n - 1:
                    a = jnp.where(valid, a, 0.0)
                parts[t] = a.astype(BF16)
                ch = ch + blk[:, SB_TILE:]
            weights.append(jnp.concatenate(parts, axis=1))
            new_carry.append(ch)
        return new_carry, acc + jnp.where(first, _dot(weights[0], vw), _dot(weights[1], vw))

    def save(sub, carry, acc):
        carry_ref[sub, 0] = carry[0]
        carry_ref[sub, 1] = carry[1]
        o_ref[rows(sub), :] = acc

    def first_windows(windows):
        zeros = jnp.zeros((SB_ROWS, LANES), F32)
        masks = {lead: kcol < qrow + lead for lead in sorted({w[3] for w in windows})}
        staged = [scores(q_heads(sub), start, n, masks[lead]) for sub, n, start, lead in windows]
        sums = [_dot(split, suffix) for _, split in staged]
        for (sub, n, start, lead), (zs, _), sm in zip(windows, staged, sums):
            save(sub, *fold(zs, sm, start, n, masks[lead], (zeros, zeros), zeros))

    def static_window(sub):
        row0 = sub * SB_ROWS
        n = min(SB_WINDOW, -(-(row0 + SB_ROWS) // SB_TILE))
        start = max(row0 + SB_ROWS - SB_WINDOW * SB_TILE, 0)
        return sub, n, start, row0 - start - (n - 1) * SB_TILE

    @pl.when(i == 0)
    def _():
        first_windows([static_window(sub) for sub in range(SB_SUBS)])

    @pl.when(i > 0)
    def _():
        first_windows([(sub, SB_WINDOW,
                        pl.multiple_of((i * SB_SUBS + sub + 1) * SB_ROWS - SB_WINDOW * SB_TILE, SB_ROWS),
                        SB_TILE - SB_ROWS)
                       for sub in range(SB_SUBS)])

    lowest = carry_ref[0, 0]
    for sub in range(SB_SUBS):
        for hd in range(2):
            lowest = jnp.minimum(lowest, carry_ref[sub, hd])

    @pl.when(jnp.min(lowest) < SB_DONE)
    def _():
        for sub in range(SB_SUBS):
            qs = q_heads(sub)

            def cond(state):
                end, lowest_carry = state
                return jnp.logical_and(end > 0, lowest_carry < SB_DONE)

            def body(state, sub=sub, qs=qs):
                end, _ = state
                start = pl.multiple_of(jnp.maximum(end - SB_TILE, 0), SB_ROWS)
                valid = kcol < end - start
                zs, split = scores(qs, start, 1, valid)
                carry, acc = fold(zs, _dot(split, suffix), start, 1, valid,
                                  (carry_ref[sub, 0], carry_ref[sub, 1]), o_ref[rows(sub), :])
                save(sub, carry, acc)
                return start, jnp.min(jnp.minimum(carry[0], carry[1]))

            folded_from = jnp.maximum((i * SB_SUBS + sub + 1) * SB_ROWS - SB_WINDOW * SB_TILE, 0)
            lax.while_loop(cond, body, (folded_from,
                                        jnp.min(jnp.minimum(carry_ref[sub, 0], carry_ref[sub, 1]))))


def _sb_attention(q, k, v, batch, seq, casts=()):
    pairs, t, _ = q.shape
    block = SB_SUBS * SB_ROWS
    nq = seq // block
    q_spec = pl.BlockSpec((None, block, LANES), lambda b, p, i: (p, b * nq + i, 0))
    kv_spec = pl.BlockSpec((None, seq, LANES), lambda b, p, i: (p, b, 0))
    (out,), copies = _call(
        _sb_kernel, "sb_attn", (batch, pairs, nq), [q_spec, kv_spec, kv_spec], [q_spec],
        [jax.ShapeDtypeStruct((pairs, t, LANES), F32)], [q, k, v], casts,
        step_of=lambda b, p, i: (b * pairs + p) * nq + i,
        scratch=[pltpu.VMEM((SB_SUBS, 2, SB_ROWS, LANES), F32)])
    return out, copies


def _mem_kv_kernel(mem_ref, g_ref, w_ref, o_ref, w16_ref):
    @pl.when(pl.program_id(0) == 0)
    def _():
        w16_ref[...] = w_ref[...].astype(BF16)

    o_ref[...] = _dot(_rms(mem_ref[...], g_ref[...]).astype(BF16), w16_ref[...]).astype(BF16)


def _mem_kv(mem, g, w_kv, casts=()):
    t, d = mem.shape
    n = w_kv.shape[1]
    (kv,), copies = _call(
        _mem_kv_kernel, "mem_kv", (t // MEM_ROWS,),
        [_rows(MEM_ROWS, d), _resident((1, d)), _resident((d, n))], [_rows(MEM_ROWS, n)],
        [jax.ShapeDtypeStruct((t, n), BF16)], [mem, g, w_kv], casts,
        scratch=[pltpu.VMEM((d, n), BF16)])
    return kv, copies


def _post_mix_kernel(h_ref, oa_ref, ob_ref, sbg_ref, wout_ref, mpost_ref, xpre_ref, wq_ref,
                     kv_ref, wo_ref, xpost_ref, o_ref):
    tm = h_ref.shape[0] // POST_PARTS
    parts = [slice(p * tm, (p + 1) * tm) for p in range(POST_PARTS)]
    heads = [slice(hd * XA_HEAD_DIM, (hd + 1) * XA_HEAD_DIM) for hd in range(XA_HEADS)]

    def merge(rs, st):
        ob = jnp.concatenate([ob_ref[p, rs, :] for p in range(SB_PAIRS)], axis=1)
        ob = _rms(ob, sbg_ref[...]).astype(BF16)
        st["merged"] = _dot(oa_ref[rs, :], wout_ref[:SGU_WIDTH, :]) + _dot(ob, wout_ref[SGU_WIDTH:, :])

    def query(rs, st):
        st["h"] = h_ref[rs, :] + _rms(st.pop("merged"), mpost_ref[...])
        x = _rms(st["h"], xpre_ref[...]).astype(BF16)
        st["q"] = (_dot(x, wq_ref[...]) * (XA_HEAD_DIM ** -0.5)).astype(BF16)

    def score(rs, st):
        q = st.pop("q")
        st["logits"] = [_dot_nt(q[:, hd], kv_ref[:, hd]) for hd in heads]
    def attend(rs, st):
        outs = []
        for hd, lg in zip(heads, st.pop("logits")):
            e = jnp.exp(lg - jnp.max(lg, axis=-1, keepdims=True))
            o = _dot(e.astype(BF16), kv_ref[:, D_MODEL + hd.start:D_MODEL + hd.stop])
            outs.append((o / jnp.sum(e, axis=-1, keepdims=True)).astype(BF16))
        st["attended"] = jnp.concatenate(outs, axis=1)

    def project(rs, st):
        st["c"] = _dot(st.pop("attended"), wo_ref[...])

    def finish(rs, st):
        o_ref[rs, :] = st.pop("h") + _rms(st.pop("c"), xpost_ref[...])

    stages = [merge, query, score, attend, project, finish]
    states = [{} for _ in parts]
    for slot in range(len(parts) + len(stages) - 1):
        for s, stage in enumerate(stages):
            p = slot - s
            if 0 <= p < len(parts):
                stage(parts[p], states[p])


def _post_mix(h, out_a, out_b, sb_g, w_out, mix_post_g, xa_pre_g, w_q, kv, w_o, xa_post_g, seq):
    t, d = h.shape
    per_batch = seq // POST_ROWS
    vec = _resident((1, d))
    (out,), _ = _call(
        _post_mix_kernel, "post_mix", (t // POST_ROWS,),
        [_rows(POST_ROWS, d), _rows(POST_ROWS, SGU_WIDTH),
         pl.BlockSpec((SB_PAIRS, POST_ROWS, LANES), lambda i: (0, i, 0)),
         _resident((1, SB_WIDTH)), _resident((d, d)), vec, vec, _resident((d, d)),
         pl.BlockSpec((MEM_LEN, 2 * d), lambda i: (i // per_batch, 0)),
         _resident((d, d)), vec],
        [_rows(POST_ROWS, d)], [jax.ShapeDtypeStruct((t, d), F32)],
        [h, out_a, out_b, sb_g, w_out, mix_post_g, xa_pre_g, w_q, kv, w_o, xa_post_g])
    return out


def kernel(x, mem, ffn1_pre_g, ffn1_post_g, ffn1_w_gate, ffn1_w_up, ffn1_w_down, mix_pre_g, mix_post_g, w_in, sgu_norm_g, sgu_norm_b, sgu_w_s, sgu_b_s, sgu_out_g, sb_out_g, w_out, xa_pre_g, xa_post_g, mem_norm_g, xa_w_q, xa_w_kv, xa_w_o, ffn2_pre_g, ffn2_post_g, ffn2_w_gate, ffn2_w_up, ffn2_w_down, final_norm_g):
    batch, seq, d = x.shape
    depth = ffn1_pre_g.shape[0]
    h = x.reshape(batch * seq, d)
    mem2 = mem.reshape(batch * mem.shape[1], d)

    for l in range(depth):
        kv, (gate16, up16, down16) = _mem_kv(
            mem2, mem_norm_g[l:l + 1], xa_w_kv[l],
            casts=[ffn1_w_gate[l], ffn1_w_up[l], ffn1_w_down[l]])
        h, (w_in16,) = _ffn(h, ffn1_pre_g[l:l + 1], ffn1_post_g[l:l + 1], gate16, up16, down16,
                            casts=[w_in[l]])
        (out_a, q, k, v), (w_out16, w_q16, w_o16, gate16, up16, down16) = _mix_in(
            h, mix_pre_g[l:l + 1], w_in16, sgu_norm_g[l], sgu_norm_b[l],
            sgu_w_s[l], sgu_b_s[l][:, :, None], sgu_out_g[l:l + 1],
            casts=[w_out[l], xa_w_q[l], xa_w_o[l], ffn2_w_gate[l], ffn2_w_up[l], ffn2_w_down[l]])
        out_b, _ = _sb_attention(q, k, v, batch, seq)
        h = _post_mix(h, out_a, out_b, sb_out_g[l:l + 1], w_out16, mix_post_g[l:l + 1],
                      xa_pre_g[l:l + 1], w_q16, kv, w_o16, xa_post_g[l:l + 1], seq)
        h, _ = _ffn(h, ffn2_pre_g[l:l + 1], ffn2_post_g[l:l + 1], gate16, up16, down16,
                    final_g=final_norm_g[l:l + 1])
    return h.reshape(batch, seq, d)
```

```python
import functools

import jax
import jax.numpy as jnp
from jax import lax
from jax.experimental import pallas as pl
from jax.experimental.pallas import tpu as pltpu

D_MODEL = 1024
MEM_LEN = 256
SGU_WIDTH = 512
SGU_GROUPS = 4
SGU_GROUP_DIM = SGU_WIDTH // SGU_GROUPS
CHUNK = 128
SB_WIDTH = 512
SB_HEAD_DIM = 64
XA_HEADS = 4
XA_HEAD_DIM = D_MODEL // XA_HEADS
D_FF = 2816
IN_COLS = 2 * SGU_WIDTH + 3 * SB_WIDTH
EPS = 1e-6

LANES = 128
SB_PAIRS = SB_WIDTH // LANES
BF16_ROWS = 16
VMEM_LIMIT = 56 * 1024 * 1024

FFN_ROWS = 1024
FFN_PARTS = 4
FFN_TILE = 256
MIX_ROWS = 1024
MIX_PARTS = 4
MEM_ROWS = 256
POST_ROWS = 1024
POST_PARTS = 4
SB_TILE = 128
SB_ROWS = 64
SB_SUBS = 16
SB_WINDOW = 2
SB_LAG = 2
SB_DONE = 105.0
SB_CLAMP = 80.0

F32 = jnp.float32
BF16 = jnp.bfloat16


def _rms(x, g):
    return x * lax.rsqrt(jnp.mean(x * x, axis=-1, keepdims=True) + EPS) * g


def _dot(a, b):
    return jnp.dot(a, b, preferred_element_type=F32)


def _dot_nt(a, b):
    return lax.dot_general(a, b, (((1,), (1,)), ((), ())), preferred_element_type=F32)


def _rows(tm, width):
    return pl.BlockSpec((tm, width), lambda i: (i, 0))


def _resident(shape):
    return pl.BlockSpec(shape, lambda *_: (0,) * len(shape), pipeline_mode=pl.Buffered(1))


def _params(*semantics):
    return pltpu.CompilerParams(dimension_semantics=semantics, vmem_limit_bytes=VMEM_LIMIT)


def _cast_spec(w, steps, step_of):
    rows, cols = w.shape
    n = max(c for c in range(1, steps + 1) if rows % c == 0 and (rows // c) % BF16_ROWS == 0)
    return pl.BlockSpec((rows // n, cols), lambda *idx: (jnp.minimum(step_of(*idx), n - 1), 0))


def _call(body, name, grid, in_specs, out_specs, out_shape, args, casts=(), step_of=lambda i: i,
          scratch=()):
    steps = 1
    for g in grid:
        steps *= g
    n_in, n_out, n_cast = len(args), len(out_specs), len(casts)
    cast_specs = [_cast_spec(w, steps, step_of) for w in casts]

    def kernel(*refs):
        chunks = refs[n_in:n_in + n_cast]
        copies = refs[n_in + n_cast + n_out:n_in + 2 * n_cast + n_out]
        for w_ref, w16_ref in zip(chunks, copies):
            w16_ref[...] = w_ref[...].astype(BF16)
        body(*refs[:n_in], *refs[n_in + n_cast:n_in + n_cast + n_out], *refs[n_in + 2 * n_cast + n_out:])

    outs = pl.pallas_call(
        kernel,
        grid=grid,
        in_specs=list(in_specs) + cast_specs,
        out_specs=list(out_specs) + cast_specs,
        out_shape=list(out_shape) + [jax.ShapeDtypeStruct(w.shape, BF16) for w in casts],
        scratch_shapes=list(scratch),
        compiler_params=_params(*["arbitrary"] * len(grid)),
        name=name,
    )(*args, *casts)
    return outs[:n_out], outs[n_out:]


def _ffn_kernel(h_ref, pre_ref, post_ref, wg_ref, wu_ref, wd_ref, *rest, final_norm):
    o_ref = rest[-1]
    tm = h_ref.shape[0] // FFN_PARTS
    parts = [slice(p * tm, (p + 1) * tm) for p in range(FFN_PARTS)]

    def expand(rs):
        x = _rms(h_ref[rs, :], pre_ref[...]).astype(BF16)
        act = []
        for lo in range(0, wg_ref.shape[1], FFN_TILE):
            gate = _dot(x, wg_ref[:, lo:lo + FFN_TILE])
            up = _dot(x, wu_ref[:, lo:lo + FFN_TILE])
            act.append((gate * jax.nn.sigmoid(gate) * up).astype(BF16))
        return jnp.concatenate(act, axis=1)

    def contract(act):
        return _dot(act, wd_ref[...])

    def finish(rs, f):
        y = h_ref[rs, :] + 0.5 * _rms(f, post_ref[...])
        if final_norm:
            y = _rms(y, rest[0][...])
        o_ref[rs, :] = y

    hidden = expand(parts[0])
    for p in range(FFN_PARTS):
        nxt = expand(parts[p + 1]) if p + 1 < FFN_PARTS else None
        f = contract(hidden)
        if p > 0:
            finish(parts[p - 1], prev)
        prev, hidden = f, nxt
    finish(parts[-1], prev)


def _ffn(h, pre_g, post_g, w_gate, w_up, w_down, final_g=None, casts=()):
    t, d = h.shape
    f = w_gate.shape[1]
    vec = _resident((1, d))
    in_specs = [_rows(FFN_ROWS, d), vec, vec, _resident((d, f)), _resident((d, f)), _resident((f, d))]
    args = [h, pre_g, post_g, w_gate, w_up, w_down]
    if final_g is not None:
        in_specs.append(vec)
        args.append(final_g)
    (out,), copies = _call(
        functools.partial(_ffn_kernel, final_norm=final_g is not None),
        "ffn_final" if final_g is not None else "ffn",
        (t // FFN_ROWS,), in_specs, [_rows(FFN_ROWS, d)], [jax.ShapeDtypeStruct((t, d), F32)],
        args, casts)
    return out, copies


def _mix_in_kernel(h_ref, g_ref, win_ref, ng_ref, nb_ref, ws_ref, bs_ref, og_ref,
                   oa_ref, q_ref, k_ref, v_ref):
    tm = h_ref.shape[0] // MIX_PARTS
    row = lax.broadcasted_iota(jnp.int32, (CHUNK, CHUNK), 0)
    col = lax.broadcasted_iota(jnp.int32, (CHUNK, CHUNK), 1)
    ws = [jnp.where(col <= row, ws_ref[g], 0.0).astype(BF16) for g in range(SGU_GROUPS)]

    def project(rs):
        return _dot(_rms(h_ref[rs, :], g_ref[...]).astype(BF16), win_ref[...])

    def gate(rs, proj):
        groups = []
        ss = jnp.zeros((tm, 1), F32)
        for g in range(SGU_GROUPS):
            lo = g * SGU_GROUP_DIM
            u = jax.nn.gelu(proj[:, lo:lo + SGU_GROUP_DIM])
            vg = jax.nn.gelu(proj[:, SGU_WIDTH + lo:SGU_WIDTH + lo + SGU_GROUP_DIM])
            mu = jnp.mean(vg, axis=-1, keepdims=True)
            dv = vg - mu
            var = jnp.mean(dv * dv, axis=-1, keepdims=True)
            vn = (dv * lax.rsqrt(var + EPS) * ng_ref[g:g + 1, :] + nb_ref[g:g + 1, :]).astype(BF16)
            chunks = [vn[c * CHUNK:(c + 1) * CHUNK, :] for c in range(tm // CHUNK)]
            wide = _dot(ws[g], jnp.concatenate(chunks, axis=1))
            mixed = jnp.concatenate(
                [wide[:, c * SGU_GROUP_DIM:(c + 1) * SGU_GROUP_DIM] + bs_ref[g]
                 for c in range(tm // CHUNK)], axis=0)
            oa = u * mixed
            ss = ss + jnp.sum(oa * oa, axis=-1, keepdims=True)
            groups.append(oa)
        inv = lax.rsqrt(ss * (1.0 / SGU_WIDTH) + EPS)
        for g in range(SGU_GROUPS):
            lo = g * SGU_GROUP_DIM
            oa_ref[rs, lo:lo + SGU_GROUP_DIM] = (
                groups[g] * inv * og_ref[:, lo:lo + SGU_GROUP_DIM]).astype(BF16)
        base = 2 * SGU_WIDTH
        for p in range(SB_PAIRS):
            lo = base + p * LANES
            q_ref[p, rs, :] = (proj[:, lo:lo + LANES] * (SB_HEAD_DIM ** -0.5)).astype(BF16)
            k_ref[p, rs, :] = proj[:, lo + SB_WIDTH:lo + SB_WIDTH + LANES].astype(BF16)
            v_ref[p, rs, :] = proj[:, lo + 2 * SB_WIDTH:lo + 2 * SB_WIDTH + LANES].astype(BF16)

    parts = [slice(p * tm, (p + 1) * tm) for p in range(MIX_PARTS)]
    proj = project(parts[0])
    for p in range(1, MIX_PARTS):
        nxt = project(parts[p])
        gate(parts[p - 1], proj)
        proj = nxt
    gate(parts[-1], proj)


def _mix_in(h, pre_g, w_in, norm_g, norm_b, w_s, b_s, out_g, casts=()):
    t, d = h.shape
    half = jax.ShapeDtypeStruct((SB_PAIRS, t, LANES), BF16)
    half_spec = pl.BlockSpec((SB_PAIRS, MIX_ROWS, LANES), lambda i: (0, i, 0))
    return _call(
        _mix_in_kernel, "mix_in", (t // MIX_ROWS,),
        [_rows(MIX_ROWS, d), _resident((1, d)), _resident((d, IN_COLS)),
         _resident((SGU_GROUPS, SGU_GROUP_DIM)), _resident((SGU_GROUPS, SGU_GROUP_DIM)),
         _resident((SGU_GROUPS, CHUNK, CHUNK)), _resident((SGU_GROUPS, CHUNK, 1)),
         _resident((1, SGU_WIDTH))],
        [_rows(MIX_ROWS, SGU_WIDTH)] + [half_spec] * 3,
        [jax.ShapeDtypeStruct((t, SGU_WIDTH), BF16), half, half, half],
        [h, pre_g, w_in, norm_g, norm_b, w_s, b_s, out_g], casts)


def _sb_kernel(q_ref, k_ref, v_ref, o_ref, carry_ref):
    i = pl.program_id(2)
    lane = lax.broadcasted_iota(jnp.int32, (1, LANES), 1)
    first = lane < SB_HEAD_DIM

    r = lax.broadcasted_iota(jnp.int32, (SB_TILE, SB_TILE), 0)
    c = lax.broadcasted_iota(jnp.int32, (SB_TILE, SB_TILE), 1)
    suffix = jnp.concatenate([(r >= c).astype(BF16), jnp.ones((SB_TILE, SB_TILE), BF16)], axis=1)
    suffix = jnp.concatenate([suffix, suffix], axis=0)
    qrow = lax.broadcasted_iota(jnp.int32, (SB_ROWS, SB_TILE), 0)
    kcol = lax.broadcasted_iota(jnp.int32, (SB_ROWS, SB_TILE), 1)

    def rows(sub):
        return slice(sub * SB_ROWS, (sub + 1) * SB_ROWS)

    def q_heads(sub):
        q = q_ref[rows(sub), :]
        zero = jnp.zeros_like(q)
        return jnp.where(first, q, zero), jnp.where(first, zero, q)

    def scores(qs, start, n, valid):
        kw = k_ref[pl.ds(start, n * SB_TILE), :]
        zs, split = [], []
        for qh in qs:
            z = _dot_nt(qh, kw)
            for t in range(n):
                zt = z[:, t * SB_TILE:(t + 1) * SB_TILE]
                sp = jnp.maximum(zt, jnp.log(1.0 + jnp.exp(jnp.minimum(zt, SB_CLAMP))))
                if t == n - 1:
                    sp = jnp.where(valid, sp, 0.0)
                hi = sp.astype(BF16)
                lo = (sp - hi.astype(F32)).astype(BF16)
                zs.append(zt)
                split.append(jnp.concatenate([hi, lo], axis=1))
        return zs, jnp.concatenate(split, axis=0)

    def fold(zs, sums, start, n, valid, carry, acc):
        vw = v_ref[pl.ds(start, n * SB_TILE), :]
        weights, new_carry = [], []
        for hd, ch in enumerate(carry):
            parts = [None] * n
            for t in reversed(range(n)):
                blk = sums[(hd * n + t) * SB_ROWS:(hd * n + t + 1) * SB_ROWS]
                a = jnp.exp(zs[hd * n + t] - (ch + blk[:, :SB_TILE]))
                if t == n - 1:
                    a = jnp.where(valid, a, 0.0)
                parts[t] = a.astype(BF16)
                ch = ch + blk[:, SB_TILE:]
            weights.append(jnp.concatenate(parts, axis=1))
            new_carry.append(ch)
        return new_carry, acc + jnp.where(first, _dot(weights[0], vw), _dot(weights[1], vw))

    def save(sub, carry, acc):
        carry_ref[sub, 0] = carry[0]
        carry_ref[sub, 1] = carry[1]
        o_ref[rows(sub), :] = acc

    def first_windows(windows):
        zeros = jnp.zeros((SB_ROWS, LANES), F32)
        masks = {lead: kcol < qrow + lead for lead in sorted({w[3] for w in windows})}
        staged, sums = {}, {}
        for slot in range(len(windows) + 2 * SB_LAG):
            if slot < len(windows):
                sub, n, start, lead = windows[slot]
                staged[slot] = scores(q_heads(sub), start, n, masks[lead])
            if 0 <= slot - SB_LAG < len(windows):
                sums[slot - SB_LAG] = _dot(staged[slot - SB_LAG][1], suffix)
            if 0 <= slot - 2 * SB_LAG < len(windows):
                w = slot - 2 * SB_LAG
                sub, n, start, lead = windows[w]
                save(sub, *fold(staged.pop(w)[0], sums.pop(w), start, n, masks[lead],
                                (zeros, zeros), zeros))

    def static_window(sub):
        row0 = sub * SB_ROWS
        n = min(SB_WINDOW, -(-(row0 + SB_ROWS) // SB_TILE))
        start = max(row0 + SB_ROWS - SB_WINDOW * SB_TILE, 0)
        return sub, n, start, row0 - start - (n - 1) * SB_TILE

    @pl.when(i == 0)
    def _():
        first_windows([static_window(sub) for sub in range(SB_SUBS)])

    @pl.when(i > 0)
    def _():
        first_windows([(sub, SB_WINDOW,
                        pl.multiple_of((i * SB_SUBS + sub + 1) * SB_ROWS - SB_WINDOW * SB_TILE, SB_ROWS),
                        SB_TILE - SB_ROWS)
                       for sub in range(SB_SUBS)])

    lowest = carry_ref[0, 0]
    for sub in range(SB_SUBS):
        for hd in range(2):
            lowest = jnp.minimum(lowest, carry_ref[sub, hd])

    @pl.when(jnp.min(lowest) < SB_DONE)
    def _():
        for sub in range(SB_SUBS):
            qs = q_heads(sub)

            def cond(state):
                end, lowest_carry = state
                return jnp.logical_and(end > 0, lowest_carry < SB_DONE)

            def body(state, sub=sub, qs=qs):
                end, _ = state
                start = pl.multiple_of(jnp.maximum(end - SB_TILE, 0), SB_ROWS)
                valid = kcol < end - start
                zs, split = scores(qs, start, 1, valid)
                carry, acc = fold(zs, _dot(split, suffix), start, 1, valid,
                                  (carry_ref[sub, 0], carry_ref[sub, 1]), o_ref[rows(sub), :])
                save(sub, carry, acc)
                return start, jnp.min(jnp.minimum(carry[0], carry[1]))

            folded_from = jnp.maximum((i * SB_SUBS + sub + 1) * SB_ROWS - SB_WINDOW * SB_TILE, 0)
            lax.while_loop(cond, body, (folded_from,
                                        jnp.min(jnp.minimum(carry_ref[sub, 0], carry_ref[sub, 1]))))


def _sb_attention(q, k, v, batch, seq, casts=()):
    pairs, t, _ = q.shape
    block = SB_SUBS * SB_ROWS
    nq = seq // block
    q_spec = pl.BlockSpec((None, block, LANES), lambda b, p, i: (p, b * nq + i, 0))
    kv_spec = pl.BlockSpec((None, seq, LANES), lambda b, p, i: (p, b, 0))
    (out,), copies = _call(
        _sb_kernel, "sb_attn", (batch, pairs, nq), [q_spec, kv_spec, kv_spec], [q_spec],
        [jax.ShapeDtypeStruct((pairs, t, LANES), F32)], [q, k, v], casts,
        step_of=lambda b, p, i: (b * pairs + p) * nq + i,
        scratch=[pltpu.VMEM((SB_SUBS, 2, SB_ROWS, LANES), F32)])
    return out, copies


def _mem_kv_kernel(mem_ref, g_ref, w_ref, o_ref, w16_ref):
    @pl.when(pl.program_id(0) == 0)
    def _():
        w16_ref[...] = w_ref[...].astype(BF16)

    o_ref[...] = _dot(_rms(mem_ref[...], g_ref[...]).astype(BF16), w16_ref[...]).astype(BF16)


def _mem_kv(mem, g, w_kv, casts=()):
    t, d = mem.shape
    n = w_kv.shape[1]
    (kv,), copies = _call(
        _mem_kv_kernel, "mem_kv", (t // MEM_ROWS,),
        [_rows(MEM_ROWS, d), _resident((1, d)), _resident((d, n))], [_rows(MEM_ROWS, n)],
        [jax.ShapeDtypeStruct((t, n), BF16)], [mem, g, w_kv], casts,
        scratch=[pltpu.VMEM((d, n), BF16)])
    return kv, copies


def _post_mix_kernel(h_ref, oa_ref, ob_ref, sbg_ref, wout_ref, mpost_ref, xpre_ref, wq_ref,
                     kv_ref, wo_ref, xpost_ref, o_ref):
    tm = h_ref.shape[0] // POST_PARTS
    parts = [slice(p * tm, (p + 1) * tm) for p in range(POST_PARTS)]
    heads = [slice(hd * XA_HEAD_DIM, (hd + 1) * XA_HEAD_DIM) for hd in range(XA_HEADS)]

    def merge(rs, st):
        ob = jnp.concatenate([ob_ref[p, rs, :] for p in range(SB_PAIRS)], axis=1)
        ob = _rms(ob, sbg_ref[...]).astype(BF16)
        st["merged"] = _dot(oa_ref[rs, :], wout_ref[:SGU_WIDTH, :]) + _dot(ob, wout_ref[SGU_WIDTH:, :])

    def query(rs, st):
        st["h"] = h_ref[rs, :] + _rms(st.pop("merged"), mpost_ref[...])
        x = _rms(st["h"], xpre_ref[...]).astype(BF16)
        st["q"] = (_dot(x, wq_ref[...]) * (XA_HEAD_DIM ** -0.5)).astype(BF16)

    def score(rs, st):
        q = st.pop("q")
        st["logits"] = [_dot_nt(q[:, hd], kv_ref[:, hd]) for hd in heads]
    def attend(rs, st):
        outs = []
        for hd, lg in zip(heads, st.pop("logits")):
            e = jnp.exp(lg - jnp.max(lg, axis=-1, keepdims=True))
            o = _dot(e.astype(BF16), kv_ref[:, D_MODEL + hd.start:D_MODEL + hd.stop])
            outs.append((o / jnp.sum(e, axis=-1, keepdims=True)).astype(BF16))
        st["attended"] = jnp.concatenate(outs, axis=1)

    def project(rs, st):
        st["c"] = _dot(st.pop("attended"), wo_ref[...])

    def finish(rs, st):
        o_ref[rs, :] = st.pop("h") + _rms(st.pop("c"), xpost_ref[...])

    stages = [merge, query, score, attend, project, finish]
    states = [{} for _ in parts]
    for slot in range(len(parts) + len(stages) - 1):
        for s, stage in enumerate(stages):
            p = slot - s
            if 0 <= p < len(parts):
                stage(parts[p], states[p])


def _post_mix(h, out_a, out_b, sb_g, w_out, mix_post_g, xa_pre_g, w_q, kv, w_o, xa_post_g, seq):
    t, d = h.shape
    per_batch = seq // POST_ROWS
    vec = _resident((1, d))
    (out,), _ = _call(
        _post_mix_kernel, "post_mix", (t // POST_ROWS,),
        [_rows(POST_ROWS, d), _rows(POST_ROWS, SGU_WIDTH),
         pl.BlockSpec((SB_PAIRS, POST_ROWS, LANES), lambda i: (0, i, 0)),
         _resident((1, SB_WIDTH)), _resident((d, d)), vec, vec, _resident((d, d)),
         pl.BlockSpec((MEM_LEN, 2 * d), lambda i: (i // per_batch, 0)),
         _resident((d, d)), vec],
        [_rows(POST_ROWS, d)], [jax.ShapeDtypeStruct((t, d), F32)],
        [h, out_a, out_b, sb_g, w_out, mix_post_g, xa_pre_g, w_q, kv, w_o, xa_post_g])
    return out


def kernel(x, mem, ffn1_pre_g, ffn1_post_g, ffn1_w_gate, ffn1_w_up, ffn1_w_down, mix_pre_g, mix_post_g, w_in, sgu_norm_g, sgu_norm_b, sgu_w_s, sgu_b_s, sgu_out_g, sb_out_g, w_out, xa_pre_g, xa_post_g, mem_norm_g, xa_w_q, xa_w_kv, xa_w_o, ffn2_pre_g, ffn2_post_g, ffn2_w_gate, ffn2_w_up, ffn2_w_down, final_norm_g):
    batch, seq, d = x.shape
    depth = ffn1_pre_g.shape[0]
    h = x.reshape(batch * seq, d)
    mem2 = mem.reshape(batch * mem.shape[1], d)

    for l in range(depth):
        kv, (gate16, up16, down16) = _mem_kv(
            mem2, mem_norm_g[l:l + 1], xa_w_kv[l],
            casts=[ffn1_w_gate[l], ffn1_w_up[l], ffn1_w_down[l]])
        h, (w_in16,) = _ffn(h, ffn1_pre_g[l:l + 1], ffn1_post_g[l:l + 1], gate16, up16, down16,
                            casts=[w_in[l]])
        (out_a, q, k, v), (w_out16, w_q16, w_o16, gate16, up16, down16) = _mix_in(
            h, mix_pre_g[l:l + 1], w_in16, sgu_norm_g[l], sgu_norm_b[l],
            sgu_w_s[l], sgu_b_s[l][:, :, None], sgu_out_g[l:l + 1],
            casts=[w_out[l], xa_w_q[l], xa_w_o[l], ffn2_w_gate[l], ffn2_w_up[l], ffn2_w_down[l]])
        out_b, _ = _sb_attention(q, k, v, batch, seq)
        h = _post_mix(h, out_a, out_b, sb_out_g[l:l + 1], w_out16, mix_post_g[l:l + 1],
                      xa_pre_g[l:l + 1], w_q16, kv, w_o16, xa_post_g[l:l + 1], seq)
        h, _ = _ffn(h, ffn2_pre_g[l:l + 1], ffn2_post_g[l:l + 1], gate16, up16, down16,
                    final_g=final_norm_g[l:l + 1])
    return h.reshape(batch, seq, d)
```

```python
import functools

import jax
import jax.numpy as jnp
from jax import lax
from jax.experimental import pallas as pl
from jax.experimental.pallas import tpu as pltpu

D_MODEL = 1024
MEM_LEN = 256
SGU_WIDTH = 512
SGU_GROUPS = 4
SGU_GROUP_DIM = SGU_WIDTH // SGU_GROUPS
CHUNK = 128
SB_WIDTH = 512
SB_HEAD_DIM = 64
XA_HEADS = 4
XA_HEAD_DIM = D_MODEL // XA_HEADS
D_FF = 2816
IN_COLS = 2 * SGU_WIDTH + 3 * SB_WIDTH
EPS = 1e-6

LANES = 128
SB_PAIRS = SB_WIDTH // LANES
BF16_ROWS = 16
VMEM_LIMIT = 56 * 1024 * 1024

FFN_ROWS = 1024
FFN_PARTS = 4
FFN_TILE = 256
MIX_ROWS = 1024
MIX_PARTS = 4
POST_ROWS = 1024
POST_PARTS = 4
SB_TILE = 128
SB_ROWS = 64
SB_SUBS = 16
SB_WINDOW = 2
SB_DONE = 105.0
SB_CLAMP = 80.0

F32 = jnp.float32
BF16 = jnp.bfloat16


def _rms(x, g):
    return x * lax.rsqrt(jnp.mean(x * x, axis=-1, keepdims=True) + EPS) * g


def _dot(a, b):
    return jnp.dot(a, b, preferred_element_type=F32)


def _dot_nt(a, b):
    return lax.dot_general(a, b, (((1,), (1,)), ((), ())), preferred_element_type=F32)


def _rows(tm, width):
    return pl.BlockSpec((tm, width), lambda i: (i, 0))


def _resident(shape):
    return pl.BlockSpec(shape, lambda *_: (0,) * len(shape), pipeline_mode=pl.Buffered(1))


def _params(*semantics):
    return pltpu.CompilerParams(dimension_semantics=semantics, vmem_limit_bytes=VMEM_LIMIT)


def _cast_spec(w, steps, step_of):
    rows, cols = w.shape
    n = max(c for c in range(1, steps + 1) if rows % c == 0 and (rows // c) % BF16_ROWS == 0)
    return pl.BlockSpec((rows // n, cols), lambda *idx: (jnp.minimum(step_of(*idx), n - 1), 0))


def _call(body, name, grid, in_specs, out_specs, out_shape, args, casts=(), step_of=lambda i: i,
          scratch=()):
    steps = 1
    for g in grid:
        steps *= g
    n_in, n_out, n_cast = len(args), len(out_specs), len(casts)
    cast_specs = [_cast_spec(w, steps, step_of) for w in casts]

    def kernel(*refs):
        chunks = refs[n_in:n_in + n_cast]
        copies = refs[n_in + n_cast + n_out:n_in + 2 * n_cast + n_out]
        for w_ref, w16_ref in zip(chunks, copies):
            w16_ref[...] = w_ref[...].astype(BF16)
        body(*refs[:n_in], *refs[n_in + n_cast:n_in + n_cast + n_out], *refs[n_in + 2 * n_cast + n_out:])

    outs = pl.pallas_call(
        kernel,
        grid=grid,
        in_specs=list(in_specs) + cast_specs,
        out_specs=list(out_specs) + cast_specs,
        out_shape=list(out_shape) + [jax.ShapeDtypeStruct(w.shape, BF16) for w in casts],
        scratch_shapes=list(scratch),
        compiler_params=_params(*["arbitrary"] * len(grid)),
        name=name,
    )(*args, *casts)
    return outs[:n_out], outs[n_out:]


def _ffn_kernel(h_ref, pre_ref, post_ref, wg_ref, wu_ref, wd_ref, *rest, final_norm):
    o_ref = rest[-1]
    tm = h_ref.shape[0] // FFN_PARTS
    parts = [slice(p * tm, (p + 1) * tm) for p in range(FFN_PARTS)]

    def expand(rs):
        x = _rms(h_ref[rs, :], pre_ref[...]).astype(BF16)
        act = []
        for lo in range(0, wg_ref.shape[1], FFN_TILE):
            gate = _dot(x, wg_ref[:, lo:lo + FFN_TILE])
            up = _dot(x, wu_ref[:, lo:lo + FFN_TILE])
            act.append((gate * jax.nn.sigmoid(gate) * up).astype(BF16))
        return jnp.concatenate(act, axis=1)

    def contract(act):
        return _dot(act, wd_ref[...])

    def finish(rs, f):
        y = h_ref[rs, :] + 0.5 * _rms(f, post_ref[...])
        if final_norm:
            y = _rms(y, rest[0][...])
        o_ref[rs, :] = y

    hidden = expand(parts[0])
    for p in range(FFN_PARTS):
        nxt = expand(parts[p + 1]) if p + 1 < FFN_PARTS else None
        f = contract(hidden)
        if p > 0:
            finish(parts[p - 1], prev)
        prev, hidden = f, nxt
    finish(parts[-1], prev)


def _ffn(h, pre_g, post_g, w_gate, w_up, w_down, final_g=None, casts=()):
    t, d = h.shape
    f = w_gate.shape[1]
    vec = _resident((1, d))
    in_specs = [_rows(FFN_ROWS, d), vec, vec, _resident((d, f)), _resident((d, f)), _resident((f, d))]
    args = [h, pre_g, post_g, w_gate, w_up, w_down]
    if final_g is not None:
        in_specs.append(vec)
        args.append(final_g)
    (out,), copies = _call(
        functools.partial(_ffn_kernel, final_norm=final_g is not None),
        "ffn_final" if final_g is not None else "ffn",
        (t // FFN_ROWS,), in_specs, [_rows(FFN_ROWS, d)], [jax.ShapeDtypeStruct((t, d), F32)],
        args, casts)
    return out, copies


def _mix_in_kernel(h_ref, g_ref, win_ref, ng_ref, nb_ref, ws_ref, bs_ref, og_ref,
                   oa_ref, q_ref, k_ref, v_ref):
    tm = h_ref.shape[0] // MIX_PARTS
    row = lax.broadcasted_iota(jnp.int32, (CHUNK, CHUNK), 0)
    col = lax.broadcasted_iota(jnp.int32, (CHUNK, CHUNK), 1)
    ws = [jnp.where(col <= row, ws_ref[g], 0.0).astype(BF16) for g in range(SGU_GROUPS)]

    def project(rs):
        return _dot(_rms(h_ref[rs, :], g_ref[...]).astype(BF16), win_ref[...])

    def gate(rs, proj):
        groups = []
        ss = jnp.zeros((tm, 1), F32)
        for g in range(SGU_GROUPS):
            lo = g * SGU_GROUP_DIM
            u = jax.nn.gelu(proj[:, lo:lo + SGU_GROUP_DIM])
            vg = jax.nn.gelu(proj[:, SGU_WIDTH + lo:SGU_WIDTH + lo + SGU_GROUP_DIM])
            mu = jnp.mean(vg, axis=-1, keepdims=True)
            dv = vg - mu
            var = jnp.mean(dv * dv, axis=-1, keepdims=True)
            vn = (dv * lax.rsqrt(var + EPS) * ng_ref[g:g + 1, :] + nb_ref[g:g + 1, :]).astype(BF16)
            chunks = [vn[c * CHUNK:(c + 1) * CHUNK, :] for c in range(tm // CHUNK)]
            wide = _dot(ws[g], jnp.concatenate(chunks, axis=1))
            mixed = jnp.concatenate(
                [wide[:, c * SGU_GROUP_DIM:(c + 1) * SGU_GROUP_DIM] + bs_ref[g]
                 for c in range(tm // CHUNK)], axis=0)
            oa = u * mixed
            ss = ss + jnp.sum(oa * oa, axis=-1, keepdims=True)
            groups.append(oa)
        inv = lax.rsqrt(ss * (1.0 / SGU_WIDTH) + EPS)
        for g in range(SGU_GROUPS):
            lo = g * SGU_GROUP_DIM
            oa_ref[rs, lo:lo + SGU_GROUP_DIM] = (
                groups[g] * inv * og_ref[:, lo:lo + SGU_GROUP_DIM]).astype(BF16)
        base = 2 * SGU_WIDTH
        for p in range(SB_PAIRS):
            lo = base + p * LANES
            q_ref[p, rs, :] = (proj[:, lo:lo + LANES] * (SB_HEAD_DIM ** -0.5)).astype(BF16)
            k_ref[p, rs, :] = proj[:, lo + SB_WIDTH:lo + SB_WIDTH + LANES].astype(BF16)
            v_ref[p, rs, :] = proj[:, lo + 2 * SB_WIDTH:lo + 2 * SB_WIDTH + LANES].astype(BF16)

    parts = [slice(p * tm, (p + 1) * tm) for p in range(MIX_PARTS)]
    proj = project(parts[0])
    for p in range(1, MIX_PARTS):
        nxt = project(parts[p])
        gate(parts[p - 1], proj)
        proj = nxt
    gate(parts[-1], proj)


def _mix_in(h, pre_g, w_in, norm_g, norm_b, w_s, b_s, out_g, casts=()):
    t, d = h.shape
    half = jax.ShapeDtypeStruct((SB_PAIRS, t, LANES), BF16)
    half_spec = pl.BlockSpec((SB_PAIRS, MIX_ROWS, LANES), lambda i: (0, i, 0))
    return _call(
        _mix_in_kernel, "mix_in", (t // MIX_ROWS,),
        [_rows(MIX_ROWS, d), _resident((1, d)), _resident((d, IN_COLS)),
         _resident((SGU_GROUPS, SGU_GROUP_DIM)), _resident((SGU_GROUPS, SGU_GROUP_DIM)),
         _resident((SGU_GROUPS, CHUNK, CHUNK)), _resident((SGU_GROUPS, CHUNK, 1)),
         _resident((1, SGU_WIDTH))],
        [_rows(MIX_ROWS, SGU_WIDTH)] + [half_spec] * 3,
        [jax.ShapeDtypeStruct((t, SGU_WIDTH), BF16), half, half, half],
        [h, pre_g, w_in, norm_g, norm_b, w_s, b_s, out_g], casts)


def _sb_kernel(q_ref, k_ref, v_ref, o_ref, carry_ref):
    i = pl.program_id(2)
    lane = lax.broadcasted_iota(jnp.int32, (1, LANES), 1)
    first = lane < SB_HEAD_DIM

    r = lax.broadcasted_iota(jnp.int32, (SB_TILE, SB_TILE), 0)
    c = lax.broadcasted_iota(jnp.int32, (SB_TILE, SB_TILE), 1)
    suffix = jnp.concatenate([(r >= c).astype(BF16), jnp.ones((SB_TILE, SB_TILE), BF16)], axis=1)
    suffix = jnp.concatenate([suffix, suffix], axis=0)
    qrow = lax.broadcasted_iota(jnp.int32, (SB_ROWS, SB_TILE), 0)
    kcol = lax.broadcasted_iota(jnp.int32, (SB_ROWS, SB_TILE), 1)

    def rows(sub):
        return slice(sub * SB_ROWS, (sub + 1) * SB_ROWS)

    def q_heads(sub):
        q = q_ref[rows(sub), :]
        zero = jnp.zeros_like(q)
        return jnp.where(first, q, zero), jnp.where(first, zero, q)

    def scores(qs, start, n, valid):
        kw = k_ref[pl.ds(start, n * SB_TILE), :]
        zs, split = [], []
        for qh in qs:
            z = _dot_nt(qh, kw)
            for t in range(n):
                zt = z[:, t * SB_TILE:(t + 1) * SB_TILE]
                sp = jnp.maximum(zt, jnp.log(1.0 + jnp.exp(jnp.minimum(zt, SB_CLAMP))))
                if t == n - 1:
                    sp = jnp.where(valid, sp, 0.0)
                hi = sp.astype(BF16)
                lo = (sp - hi.astype(F32)).astype(BF16)
                zs.append(zt)
                split.append(jnp.concatenate([hi, lo], axis=1))
        return zs, jnp.concatenate(split, axis=0)

    def fold(zs, sums, start, n, valid, carry, acc):
        vw = v_ref[pl.ds(start, n * SB_TILE), :]
        weights, new_carry = [], []
        for hd, ch in enumerate(carry):
            parts = [None] * n
            for t in reversed(range(n)):
                blk = sums[(hd * n + t) * SB_ROWS:(hd * n + t + 1) * SB_ROWS]
                a = jnp.exp(zs[hd * n + t] - (ch + blk[:, :SB_TILE]))
                if t == n - 1:
                    a = jnp.where(valid, a, 0.0)
                parts[t] = a.astype(BF16)
                ch = ch + blk[:, SB_TILE:]
            weights.append(jnp.concatenate(parts, axis=1))
            new_carry.append(ch)
        return new_carry, acc + jnp.where(first, _dot(weights[0], vw), _dot(weights[1], vw))

    def save(sub, carry, acc):
        carry_ref[sub, 0] = carry[0]
        carry_ref[sub, 1] = carry[1]
        o_ref[rows(sub), :] = acc

    def first_windows(windows):
        zeros = jnp.zeros((SB_ROWS, LANES), F32)
        masks = {lead: kcol < qrow + lead for lead in sorted({w[3] for w in windows})}
        staged = [scores(q_heads(sub), start, n, masks[lead]) for sub, n, start, lead in windows]
        sums = [_dot(split, suffix) for _, split in staged]
        for (sub, n, start, lead), (zs, _), sm in zip(windows, staged, sums):
            save(sub, *fold(zs, sm, start, n, masks[lead], (zeros, zeros), zeros))

    def static_window(sub):
        row0 = sub * SB_ROWS
        n = min(SB_WINDOW, -(-(row0 + SB_ROWS) // SB_TILE))
        start = max(row0 + SB_ROWS - SB_WINDOW * SB_TILE, 0)
        return sub, n, start, row0 - start - (n - 1) * SB_TILE

    @pl.when(i == 0)
    def _():
        first_windows([static_window(sub) for sub in range(SB_SUBS)])

    @pl.when(i > 0)
    def _():
        first_windows([(sub, SB_WINDOW,
                        pl.multiple_of((i * SB_SUBS + sub + 1) * SB_ROWS - SB_WINDOW * SB_TILE, SB_ROWS),
                        SB_TILE - SB_ROWS)
                       for sub in range(SB_SUBS)])

    lowest = carry_ref[0, 0]
    for sub in range(SB_SUBS):
        for hd in range(2):
            lowest = jnp.minimum(lowest, carry_ref[sub, hd])

    @pl.when(jnp.min(lowest) < SB_DONE)
    def _():
        for sub in range(SB_SUBS):
            qs = q_heads(sub)

            def cond(state):
                end, lowest_carry = state
                return jnp.logical_and(end > 0, lowest_carry < SB_DONE)

            def body(state, sub=sub, qs=qs):
                end, _ = state
                start = pl.multiple_of(jnp.maximum(end - SB_TILE, 0), SB_ROWS)
                valid = kcol < end - start
                zs, split = scores(qs, start, 1, valid)
                carry, acc = fold(zs, _dot(split, suffix), start, 1, valid,
                                  (carry_ref[sub, 0], carry_ref[sub, 1]), o_ref[rows(sub), :])
                save(sub, carry, acc)
                return start, jnp.min(jnp.minimum(carry[0], carry[1]))

            folded_from = jnp.maximum((i * SB_SUBS + sub + 1) * SB_ROWS - SB_WINDOW * SB_TILE, 0)
            lax.while_loop(cond, body, (folded_from,
                                        jnp.min(jnp.minimum(carry_ref[sub, 0], carry_ref[sub, 1]))))


def _sb_attention(q, k, v, batch, seq, casts=()):
    pairs, t, _ = q.shape
    block = SB_SUBS * SB_ROWS
    nq = seq // block
    q_spec = pl.BlockSpec((None, block, LANES), lambda b, p, i: (p, b * nq + i, 0))
    kv_spec = pl.BlockSpec((None, seq, LANES), lambda b, p, i: (p, b, 0))
    (out,), copies = _call(
        _sb_kernel, "sb_attn", (batch, pairs, nq), [q_spec, kv_spec, kv_spec], [q_spec],
        [jax.ShapeDtypeStruct((pairs, t, LANES), F32)], [q, k, v], casts,
        step_of=lambda b, p, i: (b * pairs + p) * nq + i,
        scratch=[pltpu.VMEM((SB_SUBS, 2, SB_ROWS, LANES), F32)])
    return out, copies


def _mem_kv_kernel(mem_ref, g_ref, w_ref, kt_ref, v_ref, w16_ref):
    @pl.when(pl.program_id(0) == 0)
    def _():
        w16_ref[...] = w_ref[...].astype(BF16)

    kv = _dot(_rms(mem_ref[...], g_ref[...]).astype(BF16), w16_ref[...])
    d = kt_ref.shape[0]
    kt_ref[...] = kv[:, :d].T.astype(BF16)
    v_ref[...] = kv[:, d:].astype(BF16)


def _mem_kv(mem, g, w_kv, casts=()):
    t, d = mem.shape
    batch = t // MEM_LEN
    (kt, v), copies = _call(
        _mem_kv_kernel, "mem_kv", (batch,),
        [_rows(MEM_LEN, d), _resident((1, d)), _resident((d, 2 * d))],
        [_rows(d, MEM_LEN), _rows(MEM_LEN, d)],
        [jax.ShapeDtypeStruct((batch * d, MEM_LEN), BF16), jax.ShapeDtypeStruct((t, d), BF16)],
        [mem, g, w_kv], casts, scratch=[pltpu.VMEM((d, 2 * d), BF16)])
    return kt, v, copies


def _post_mix_kernel(h_ref, oa_ref, ob_ref, sbg_ref, wout_ref, mpost_ref, xpre_ref, wq_ref,
                     kt_ref, v_ref, wo_ref, xpost_ref, o_ref):
    tm = h_ref.shape[0] // POST_PARTS
    parts = [slice(p * tm, (p + 1) * tm) for p in range(POST_PARTS)]
    heads = [slice(hd * XA_HEAD_DIM, (hd + 1) * XA_HEAD_DIM) for hd in range(XA_HEADS)]

    def merge(rs, st):
        ob = jnp.concatenate([ob_ref[p, rs, :] for p in range(SB_PAIRS)], axis=1)
        ob = _rms(ob, sbg_ref[...]).astype(BF16)
        st["merged"] = _dot(oa_ref[rs, :], wout_ref[:SGU_WIDTH, :]) + _dot(ob, wout_ref[SGU_WIDTH:, :])

    def query(rs, st):
        st["h"] = h_ref[rs, :] + _rms(st.pop("merged"), mpost_ref[...])
        x = _rms(st["h"], xpre_ref[...]).astype(BF16)
        st["q"] = (_dot(x, wq_ref[...]) * (XA_HEAD_DIM ** -0.5)).astype(BF16)

    def score(rs, st):
        q = st.pop("q")
        st["logits"] = [_dot(q[:, hd], kt_ref[hd, :]) for hd in heads]

    def attend(rs, st):
        outs = []
        for hd, lg in zip(heads, st.pop("logits")):
            e = jnp.exp(lg - jnp.max(lg, axis=-1, keepdims=True))
            o = _dot(e.astype(BF16), v_ref[:, hd])
            outs.append((o / jnp.sum(e, axis=-1, keepdims=True)).astype(BF16))
        st["attended"] = jnp.concatenate(outs, axis=1)

    def project(rs, st):
        st["c"] = _dot(st.pop("attended"), wo_ref[...])

    def finish(rs, st):
        o_ref[rs, :] = st.pop("h") + _rms(st.pop("c"), xpost_ref[...])

    stages = [merge, query, score, attend, project, finish]
    states = [{} for _ in parts]
    for slot in range(len(parts) + len(stages) - 1):
        for s, stage in enumerate(stages):
            p = slot - s
            if 0 <= p < len(parts):
                stage(parts[p], states[p])


def _post_mix(h, out_a, out_b, sb_g, w_out, mix_post_g, xa_pre_g, w_q, kt, v, w_o, xa_post_g, seq):
    t, d = h.shape
    per_batch = seq // POST_ROWS
    vec = _resident((1, d))
    (out,), _ = _call(
        _post_mix_kernel, "post_mix", (t // POST_ROWS,),
        [_rows(POST_ROWS, d), _rows(POST_ROWS, SGU_WIDTH),
         pl.BlockSpec((SB_PAIRS, POST_ROWS, LANES), lambda i: (0, i, 0)),
         _resident((1, SB_WIDTH)), _resident((d, d)), vec, vec, _resident((d, d)),
         pl.BlockSpec((d, MEM_LEN), lambda i: (i // per_batch, 0)),
         pl.BlockSpec((MEM_LEN, d), lambda i: (i // per_batch, 0)),
         _resident((d, d)), vec],
        [_rows(POST_ROWS, d)], [jax.ShapeDtypeStruct((t, d), F32)],
        [h, out_a, out_b, sb_g, w_out, mix_post_g, xa_pre_g, w_q, kt, v, w_o, xa_post_g])
    return out


def kernel(x, mem, ffn1_pre_g, ffn1_post_g, ffn1_w_gate, ffn1_w_up, ffn1_w_down, mix_pre_g, mix_post_g, w_in, sgu_norm_g, sgu_norm_b, sgu_w_s, sgu_b_s, sgu_out_g, sb_out_g, w_out, xa_pre_g, xa_post_g, mem_norm_g, xa_w_q, xa_w_kv, xa_w_o, ffn2_pre_g, ffn2_post_g, ffn2_w_gate, ffn2_w_up, ffn2_w_down, final_norm_g):
    batch, seq, d = x.shape
    depth = ffn1_pre_g.shape[0]
    h = x.reshape(batch * seq, d)
    mem2 = mem.reshape(batch * mem.shape[1], d)

    for l in range(depth):
        mem_kt, mem_v, (gate16, up16, down16) = _mem_kv(
            mem2, mem_norm_g[l:l + 1], xa_w_kv[l],
            casts=[ffn1_w_gate[l], ffn1_w_up[l], ffn1_w_down[l]])
        h, (w_in16,) = _ffn(h, ffn1_pre_g[l:l + 1], ffn1_post_g[l:l + 1], gate16, up16, down16,
                            casts=[w_in[l]])
        (out_a, q, k, v), (w_out16, w_q16, w_o16, gate16, up16, down16) = _mix_in(
            h, mix_pre_g[l:l + 1], w_in16, sgu_norm_g[l], sgu_norm_b[l],
            sgu_w_s[l], sgu_b_s[l][:, :, None], sgu_out_g[l:l + 1],
            casts=[w_out[l], xa_w_q[l], xa_w_o[l], ffn2_w_gate[l], ffn2_w_up[l], ffn2_w_down[l]])
        out_b, _ = _sb_attention(q, k, v, batch, seq)
        h = _post_mix(h, out_a, out_b, sb_out_g[l:l + 1], w_out16, mix_post_g[l:l + 1],
                      xa_pre_g[l:l + 1], w_q16, mem_kt, mem_v, w_o16, xa_post_g[l:l + 1], seq)
        h, _ = _ffn(h, ffn2_pre_g[l:l + 1], ffn2_post_g[l:l + 1], gate16, up16, down16,
                    final_g=final_norm_g[l:l + 1])
    return h.reshape(batch, seq, d)
```

```python
import functools

import jax
import jax.numpy as jnp
from jax import lax
from jax.experimental import pallas as pl
from jax.experimental.pallas import tpu as pltpu

D_MODEL = 1024
MEM_LEN = 256
SGU_WIDTH = 512
SGU_GROUPS = 4
SGU_GROUP_DIM = SGU_WIDTH // SGU_GROUPS
CHUNK = 128
SB_WIDTH = 512
SB_HEAD_DIM = 64
XA_HEADS = 4
XA_HEAD_DIM = D_MODEL // XA_HEADS
IN_COLS = 2 * SGU_WIDTH + 3 * SB_WIDTH
EPS = 1e-6

LANES = 128
SB_PAIRS = SB_WIDTH // LANES
BF16_ROWS = 16
VMEM_LIMIT = 56 * 1024 * 1024

FFN_ROWS = 1024
FFN_PARTS = 4
FFN_TILE = 256
MIX_ROWS = 1024
MIX_PARTS = 4
MEM_ROWS = 256
POST_ROWS = 1024
POST_PARTS = 4
SB_TILE = 128
SB_ROWS = 64
SB_SUBS = 16
SB_WINDOW = 2
SB_DONE = 105.0
SB_CLAMP = 80.0

F32 = jnp.float32
BF16 = jnp.bfloat16


def _rms(x, g):
    return x * lax.rsqrt(jnp.mean(x * x, axis=-1, keepdims=True) + EPS) * g


def _dot(a, b):
    return jnp.dot(a, b, preferred_element_type=F32)


def _dot_nt(a, b):
    return lax.dot_general(a, b, (((1,), (1,)), ((), ())), preferred_element_type=F32)


def _rows(tm, width):
    return pl.BlockSpec((tm, width), lambda i: (i, 0))


def _resident(shape):
    return pl.BlockSpec(shape, lambda *_: (0,) * len(shape), pipeline_mode=pl.Buffered(1))


def _params(*semantics):
    return pltpu.CompilerParams(dimension_semantics=semantics, vmem_limit_bytes=VMEM_LIMIT)


def _cast_spec(w, steps, step_of):
    rows, cols = w.shape
    n = max(c for c in range(1, steps + 1) if rows % c == 0 and (rows // c) % BF16_ROWS == 0)
    return pl.BlockSpec((rows // n, cols), lambda *idx: (jnp.minimum(step_of(*idx), n - 1), 0))


def _call(body, name, grid, in_specs, out_specs, out_shape, args, casts=(), step_of=lambda i: i,
          scratch=()):
    steps = 1
    for g in grid:
        steps *= g
    n_in, n_out, n_cast = len(args), len(out_specs), len(casts)
    cast_specs = [_cast_spec(w, steps, step_of) for w in casts]

    def kernel(*refs):
        chunks = refs[n_in:n_in + n_cast]
        copies = refs[n_in + n_cast + n_out:n_in + 2 * n_cast + n_out]
        for w_ref, w16_ref in zip(chunks, copies):
            w16_ref[...] = w_ref[...].astype(BF16)
        body(*refs[:n_in], *refs[n_in + n_cast:n_in + n_cast + n_out], *refs[n_in + 2 * n_cast + n_out:])

    outs = pl.pallas_call(
        kernel,
        grid=grid,
        in_specs=list(in_specs) + cast_specs,
        out_specs=list(out_specs) + cast_specs,
        out_shape=list(out_shape) + [jax.ShapeDtypeStruct(w.shape, BF16) for w in casts],
        scratch_shapes=list(scratch),
        compiler_params=_params(*["arbitrary"] * len(grid)),
        name=name,
    )(*args, *casts)
    return outs[:n_out], outs[n_out:]


def _ffn_kernel(h_ref, pre_ref, post_ref, wg_ref, wu_ref, wd_ref, *rest, final_norm):
    o_ref = rest[-1]
    tm = h_ref.shape[0] // FFN_PARTS
    parts = [slice(p * tm, (p + 1) * tm) for p in range(FFN_PARTS)]

    def expand(rs):
        x = _rms(h_ref[rs, :], pre_ref[...]).astype(BF16)
        act = []
        for lo in range(0, wg_ref.shape[1], FFN_TILE):
            gate = _dot(x, wg_ref[:, lo:lo + FFN_TILE])
            up = _dot(x, wu_ref[:, lo:lo + FFN_TILE])
            act.append((gate * jax.nn.sigmoid(gate) * up).astype(BF16))
        return jnp.concatenate(act, axis=1)

    def contract(act):
        return _dot(act, wd_ref[...])

    def finish(rs, f):
        y = h_ref[rs, :] + 0.5 * _rms(f, post_ref[...])
        if final_norm:
            y = _rms(y, rest[0][...])
        o_ref[rs, :] = y

    hidden = expand(parts[0])
    for p in range(FFN_PARTS):
        nxt = expand(parts[p + 1]) if p + 1 < FFN_PARTS else None
        f = contract(hidden)
        if p > 0:
            finish(parts[p - 1], prev)
        prev, hidden = f, nxt
    finish(parts[-1], prev)


def _ffn(h, pre_g, post_g, w_gate, w_up, w_down, final_g=None, casts=()):
    t, d = h.shape
    f = w_gate.shape[1]
    vec = _resident((1, d))
    in_specs = [_rows(FFN_ROWS, d), vec, vec, _resident((d, f)), _resident((d, f)), _resident((f, d))]
    args = [h, pre_g, post_g, w_gate, w_up, w_down]
    if final_g is not None:
        in_specs.append(vec)
        args.append(final_g)
    (out,), copies = _call(
        functools.partial(_ffn_kernel, final_norm=final_g is not None),
        "ffn_final" if final_g is not None else "ffn",
        (t // FFN_ROWS,), in_specs, [_rows(FFN_ROWS, d)], [jax.ShapeDtypeStruct((t, d), F32)],
        args, casts)
    return out, copies


def _mix_in_kernel(h_ref, g_ref, win_ref, ng_ref, nb_ref, ws_ref, bs_ref, og_ref,
                   oa_ref, q_ref, k_ref, v_ref):
    tm = h_ref.shape[0] // MIX_PARTS
    row = lax.broadcasted_iota(jnp.int32, (CHUNK, CHUNK), 0)
    col = lax.broadcasted_iota(jnp.int32, (CHUNK, CHUNK), 1)
    ws = [jnp.where(col <= row, ws_ref[g], 0.0).astype(BF16) for g in range(SGU_GROUPS)]

    def project(rs):
        return _dot(_rms(h_ref[rs, :], g_ref[...]).astype(BF16), win_ref[...])

    def gate(rs, proj):
        groups = []
        ss = jnp.zeros((tm, 1), F32)
        for g in range(SGU_GROUPS):
            lo = g * SGU_GROUP_DIM
            u = jax.nn.gelu(proj[:, lo:lo + SGU_GROUP_DIM])
            vg = jax.nn.gelu(proj[:, SGU_WIDTH + lo:SGU_WIDTH + lo + SGU_GROUP_DIM])
            mu = jnp.mean(vg, axis=-1, keepdims=True)
            dv = vg - mu
            var = jnp.mean(dv * dv, axis=-1, keepdims=True)
            vn = (dv * lax.rsqrt(var + EPS) * ng_ref[g:g + 1, :] + nb_ref[g:g + 1, :]).astype(BF16)
            chunks = [vn[c * CHUNK:(c + 1) * CHUNK, :] for c in range(tm // CHUNK)]
            wide = _dot(ws[g], jnp.concatenate(chunks, axis=1))
            mixed = jnp.concatenate(
                [wide[:, c * SGU_GROUP_DIM:(c + 1) * SGU_GROUP_DIM] + bs_ref[g]
                 for c in range(tm // CHUNK)], axis=0)
            oa = u * mixed
            ss = ss + jnp.sum(oa * oa, axis=-1, keepdims=True)
            groups.append(oa)
        inv = lax.rsqrt(ss * (1.0 / SGU_WIDTH) + EPS)
        for g in range(SGU_GROUPS):
            lo = g * SGU_GROUP_DIM
            oa_ref[rs, lo:lo + SGU_GROUP_DIM] = (
                groups[g] * inv * og_ref[:, lo:lo + SGU_GROUP_DIM]).astype(BF16)
        base = 2 * SGU_WIDTH
        for p in range(SB_PAIRS):
            lo = base + p * LANES
            q_ref[p, rs, :] = (proj[:, lo:lo + LANES] * (SB_HEAD_DIM ** -0.5)).astype(BF16)
            k_ref[p, rs, :] = proj[:, lo + SB_WIDTH:lo + SB_WIDTH + LANES].astype(BF16)
            v_ref[p, rs, :] = proj[:, lo + 2 * SB_WIDTH:lo + 2 * SB_WIDTH + LANES].astype(BF16)

    parts = [slice(p * tm, (p + 1) * tm) for p in range(MIX_PARTS)]
    proj = project(parts[0])
    for p in range(1, MIX_PARTS):
        nxt = project(parts[p])
        gate(parts[p - 1], proj)
        proj = nxt
    gate(parts[-1], proj)


def _mix_in(h, pre_g, w_in, norm_g, norm_b, w_s, b_s, out_g, casts=()):
    t, d = h.shape
    half = jax.ShapeDtypeStruct((SB_PAIRS, t, LANES), BF16)
    half_spec = pl.BlockSpec((SB_PAIRS, MIX_ROWS, LANES), lambda i: (0, i, 0))
    return _call(
        _mix_in_kernel, "mix_in", (t // MIX_ROWS,),
        [_rows(MIX_ROWS, d), _resident((1, d)), _resident((d, IN_COLS)),
         _resident((SGU_GROUPS, SGU_GROUP_DIM)), _resident((SGU_GROUPS, SGU_GROUP_DIM)),
         _resident((SGU_GROUPS, CHUNK, CHUNK)), _resident((SGU_GROUPS, CHUNK, 1)),
         _resident((1, SGU_WIDTH))],
        [_rows(MIX_ROWS, SGU_WIDTH)] + [half_spec] * 3,
        [jax.ShapeDtypeStruct((t, SGU_WIDTH), BF16), half, half, half],
        [h, pre_g, w_in, norm_g, norm_b, w_s, b_s, out_g], casts)


def _sb_kernel(q_ref, k_ref, v_ref, o_ref, carry_ref):
    i = pl.program_id(2)
    lane = lax.broadcasted_iota(jnp.int32, (1, LANES), 1)
    first = lane < SB_HEAD_DIM

    r = lax.broadcasted_iota(jnp.int32, (SB_TILE, SB_TILE), 0)
    c = lax.broadcasted_iota(jnp.int32, (SB_TILE, SB_TILE), 1)
    suffix = jnp.concatenate([(r >= c).astype(BF16), jnp.ones((SB_TILE, SB_TILE), BF16)], axis=1)
    suffix = jnp.concatenate([suffix, suffix], axis=0)
    qrow = lax.broadcasted_iota(jnp.int32, (SB_ROWS, SB_TILE), 0)
    kcol = lax.broadcasted_iota(jnp.int32, (SB_ROWS, SB_TILE), 1)

    def rows(sub):
        return slice(sub * SB_ROWS, (sub + 1) * SB_ROWS)

    def q_heads(sub):
        q = q_ref[rows(sub), :]
        zero = jnp.zeros_like(q)
        return jnp.where(first, q, zero), jnp.where(first, zero, q)

    def scores(qs, start, n, valid):
        kw = k_ref[pl.ds(start, n * SB_TILE), :]
        zs, split = [], []
        for qh in qs:
            z = _dot_nt(qh, kw)
            for t in range(n):
                zt = z[:, t * SB_TILE:(t + 1) * SB_TILE]
                sp = jnp.maximum(zt, jnp.log(1.0 + jnp.exp(jnp.minimum(zt, SB_CLAMP))))
                if t == n - 1:
                    sp = jnp.where(valid, sp, 0.0)
                hi = sp.astype(BF16)
                lo = (sp - hi.astype(F32)).astype(BF16)
                zs.append(zt)
                split.append(jnp.concatenate([hi, lo], axis=1))
        return zs, jnp.concatenate(split, axis=0)

    def fold(zs, sums, start, n, valid, carry, acc):
        vw = v_ref[pl.ds(start, n * SB_TILE), :]
        weights, new_carry = [], []
        for hd, ch in enumerate(carry):
            parts = [None] * n
            for t in reversed(range(n)):
                blk = sums[(hd * n + t) * SB_ROWS:(hd * n + t + 1) * SB_ROWS]
                a = jnp.exp(zs[hd * n + t] - (ch + blk[:, :SB_TILE]))
                if t == n - 1:
                    a = jnp.where(valid, a, 0.0)
                parts[t] = a.astype(BF16)
                ch = ch + blk[:, SB_TILE:]
            weights.append(jnp.concatenate(parts, axis=1))
            new_carry.append(ch)
        return new_carry, acc + jnp.where(first, _dot(weights[0], vw), _dot(weights[1], vw))

    def save(sub, carry, acc):
        carry_ref[sub, 0] = carry[0]
        carry_ref[sub, 1] = carry[1]
        o_ref[rows(sub), :] = acc

    def first_windows(windows):
        zeros = jnp.zeros((SB_ROWS, LANES), F32)
        masks = {lead: kcol < qrow + lead for lead in sorted({w[3] for w in windows})}
        staged = [scores(q_heads(sub), start, n, masks[lead]) for sub, n, start, lead in windows]
        sums = [_dot(split, suffix) for _, split in staged]
        for (sub, n, start, lead), (zs, _), sm in zip(windows, staged, sums):
            save(sub, *fold(zs, sm, start, n, masks[lead], (zeros, zeros), zeros))

    def static_window(sub):
        row0 = sub * SB_ROWS
        n = min(SB_WINDOW, -(-(row0 + SB_ROWS) // SB_TILE))
        start = max(row0 + SB_ROWS - SB_WINDOW * SB_TILE, 0)
        return sub, n, start, row0 - start - (n - 1) * SB_TILE

    @pl.when(i == 0)
    def _():
        first_windows([static_window(sub) for sub in range(SB_SUBS)])

    @pl.when(i > 0)
    def _():
        first_windows([(sub, SB_WINDOW,
                        pl.multiple_of((i * SB_SUBS + sub + 1) * SB_ROWS - SB_WINDOW * SB_TILE, SB_ROWS),
                        SB_TILE - SB_ROWS)
                       for sub in range(SB_SUBS)])

    lowest = carry_ref[0, 0]
    for sub in range(SB_SUBS):
        for hd in range(2):
            lowest = jnp.minimum(lowest, carry_ref[sub, hd])

    @pl.when(jnp.min(lowest) < SB_DONE)
    def _():
        for sub in range(SB_SUBS):
            qs = q_heads(sub)

            def cond(state):
                end, lowest_carry = state
                return jnp.logical_and(end > 0, lowest_carry < SB_DONE)

            def body(state, sub=sub, qs=qs):
                end, _ = state
                start = pl.multiple_of(jnp.maximum(end - SB_TILE, 0), SB_ROWS)
                valid = kcol < end - start
                zs, split = scores(qs, start, 1, valid)
                carry, acc = fold(zs, _dot(split, suffix), start, 1, valid,
                                  (carry_ref[sub, 0], carry_ref[sub, 1]), o_ref[rows(sub), :])
                save(sub, carry, acc)
                return start, jnp.min(jnp.minimum(carry[0], carry[1]))

            folded_from = jnp.maximum((i * SB_SUBS + sub + 1) * SB_ROWS - SB_WINDOW * SB_TILE, 0)
            lax.while_loop(cond, body, (folded_from,
                                        jnp.min(jnp.minimum(carry_ref[sub, 0], carry_ref[sub, 1]))))


def _sb_attention(q, k, v, batch, seq, casts=()):
    pairs, t, _ = q.shape
    block = SB_SUBS * SB_ROWS
    nq = seq // block
    q_spec = pl.BlockSpec((None, block, LANES), lambda b, p, i: (p, b * nq + i, 0))
    kv_spec = pl.BlockSpec((None, seq, LANES), lambda b, p, i: (p, b, 0))
    (out,), copies = _call(
        _sb_kernel, "sb_attn", (batch, pairs, nq), [q_spec, kv_spec, kv_spec], [q_spec],
        [jax.ShapeDtypeStruct((pairs, t, LANES), F32)], [q, k, v], casts,
        step_of=lambda b, p, i: (b * pairs + p) * nq + i,
        scratch=[pltpu.VMEM((SB_SUBS, 2, SB_ROWS, LANES), F32)])
    return out, copies


def _mem_kv_kernel(mem_ref, g_ref, w_ref, o_ref, w16_ref):
    @pl.when(pl.program_id(0) == 0)
    def _():
        w16_ref[...] = w_ref[...].astype(BF16)

    o_ref[...] = _dot(_rms(mem_ref[...], g_ref[...]).astype(BF16), w16_ref[...]).astype(BF16)


def _mem_kv(mem, g, w_kv, casts=()):
    t, d = mem.shape
    n = w_kv.shape[1]
    (kv,), copies = _call(
        _mem_kv_kernel, "mem_kv", (t // MEM_ROWS,),
        [_rows(MEM_ROWS, d), _resident((1, d)), _resident((d, n))], [_rows(MEM_ROWS, n)],
        [jax.ShapeDtypeStruct((t, n), BF16)], [mem, g, w_kv], casts,
        scratch=[pltpu.VMEM((d, n), BF16)])
    return kv, copies


def _post_mix_kernel(h_ref, oa_ref, ob_ref, sbg_ref, wout_ref, mpost_ref, xpre_ref, wq_ref,
                     kv_ref, wo_ref, xpost_ref, o_ref):
    tm = h_ref.shape[0] // POST_PARTS
    parts = [slice(p * tm, (p + 1) * tm) for p in range(POST_PARTS)]
    heads = [slice(hd * XA_HEAD_DIM, (hd + 1) * XA_HEAD_DIM) for hd in range(XA_HEADS)]

    def merge(rs, st):
        ob = jnp.concatenate([ob_ref[p, rs, :] for p in range(SB_PAIRS)], axis=1)
        ob = _rms(ob, sbg_ref[...]).astype(BF16)
        st["merged"] = _dot(oa_ref[rs, :], wout_ref[:SGU_WIDTH, :]) + _dot(ob, wout_ref[SGU_WIDTH:, :])

    def query(rs, st):
        st["h"] = h_ref[rs, :] + _rms(st.pop("merged"), mpost_ref[...])
        x = _rms(st["h"], xpre_ref[...]).astype(BF16)
        st["q"] = (_dot(x, wq_ref[...]) * (XA_HEAD_DIM ** -0.5)).astype(BF16)

    def score(rs, st):
        q = st.pop("q")
        st["logits"] = [_dot_nt(q[:, hd], kv_ref[:, hd]) for hd in heads]

    def attend(rs, st):
        outs = []
        for hd, lg in zip(heads, st.pop("logits")):
            e = jnp.exp(lg - jnp.max(lg, axis=-1, keepdims=True))
            o = _dot(e.astype(BF16), kv_ref[:, D_MODEL + hd.start:D_MODEL + hd.stop])
            outs.append((o / jnp.sum(e, axis=-1, keepdims=True)).astype(BF16))
        st["attended"] = jnp.concatenate(outs, axis=1)

    def project(rs, st):
        st["c"] = _dot(st.pop("attended"), wo_ref[...])

    def finish(rs, st):
        o_ref[rs, :] = st.pop("h") + _rms(st.pop("c"), xpost_ref[...])

    stages = [merge, query, score, attend, project, finish]
    states = [{} for _ in parts]
    for slot in range(len(parts) + len(stages) - 1):
        for s, stage in enumerate(stages):
            p = slot - s
            if 0 <= p < len(parts):
                stage(parts[p], states[p])


def _post_mix(h, out_a, out_b, sb_g, w_out, mix_post_g, xa_pre_g, w_q, kv, w_o, xa_post_g, seq):
    t, d = h.shape
    per_batch = seq // POST_ROWS
    vec = _resident((1, d))
    (out,), _ = _call(
        _post_mix_kernel, "post_mix", (t // POST_ROWS,),
        [_rows(POST_ROWS, d), _rows(POST_ROWS, SGU_WIDTH),
         pl.BlockSpec((SB_PAIRS, POST_ROWS, LANES), lambda i: (0, i, 0)),
         _resident((1, SB_WIDTH)), _resident((d, d)), vec, vec, _resident((d, d)),
         pl.BlockSpec((MEM_LEN, 2 * d), lambda i: (i // per_batch, 0)),
         _resident((d, d)), vec],
        [_rows(POST_ROWS, d)], [jax.ShapeDtypeStruct((t, d), F32)],
        [h, out_a, out_b, sb_g, w_out, mix_post_g, xa_pre_g, w_q, kv, w_o, xa_post_g])
    return out


def kernel(x, mem, ffn1_pre_g, ffn1_post_g, ffn1_w_gate, ffn1_w_up, ffn1_w_down, mix_pre_g, mix_post_g, w_in, sgu_norm_g, sgu_norm_b, sgu_w_s, sgu_b_s, sgu_out_g, sb_out_g, w_out, xa_pre_g, xa_post_g, mem_norm_g, xa_w_q, xa_w_kv, xa_w_o, ffn2_pre_g, ffn2_post_g, ffn2_w_gate, ffn2_w_up, ffn2_w_down, final_norm_g):
    batch, seq, d = x.shape
    depth = ffn1_pre_g.shape[0]
    h = x.reshape(batch * seq, d)
    mem2 = mem.reshape(batch * mem.shape[1], d)

    for l in range(depth):
        kv, (gate16, up16, down16) = _mem_kv(
            mem2, mem_norm_g[l:l + 1], xa_w_kv[l],
            casts=[ffn1_w_gate[l], ffn1_w_up[l], ffn1_w_down[l]])
        h, (w_in16,) = _ffn(h, ffn1_pre_g[l:l + 1], ffn1_post_g[l:l + 1], gate16, up16, down16,
                            casts=[w_in[l]])
        (out_a, q, k, v), (w_out16, w_q16, w_o16, gate16, up16, down16) = _mix_in(
            h, mix_pre_g[l:l + 1], w_in16, sgu_norm_g[l], sgu_norm_b[l],
            sgu_w_s[l], sgu_b_s[l][:, :, None], sgu_out_g[l:l + 1],
            casts=[w_out[l], xa_w_q[l], xa_w_o[l], ffn2_w_gate[l], ffn2_w_up[l], ffn2_w_down[l]])
        out_b, _ = _sb_attention(q, k, v, batch, seq)
        h = _post_mix(h, out_a, out_b, sb_out_g[l:l + 1], w_out16, mix_post_g[l:l + 1],
                      xa_pre_g[l:l + 1], w_q16, kv, w_o16, xa_post_g[l:l + 1], seq)
        h, _ = _ffn(h, ffn2_pre_g[l:l + 1], ffn2_post_g[l:l + 1], gate16, up16, down16,
                    final_g=final_norm_g[l:l + 1])
    return h.reshape(batch, seq, d)
```
